```python
import math
import jax, jax.numpy as jnp
from jax import lax
import numpy as np

D_MODEL = 1024
BATCH = 16
SEQ = 2048
DEPTH = 4

D_MIX = D_MODEL
N_GROUPS = 4
GROUP_W = D_MIX // N_GROUPS
N_HEADS = 4
A_DK = 64
A_DV = GROUP_W // N_HEADS
B_DK = 64
B_DV = GROUP_W // N_HEADS
CONV_W = 4
CONV_CH = 2 * N_HEADS * B_DK + GROUP_W
C_D = GROUP_W // N_HEADS
Q_BLOCK = 128
D_DK = 32
D_DV = GROUP_W // N_HEADS
ROPE_BASE = 10000.0
CHUNK = 64
D_FF = 2816
EPS = 1e-6

MIX_IN_SPLITS = (
    N_HEADS * A_DK, N_HEADS * A_DK, GROUP_W, GROUP_W,
    N_HEADS * B_DK, N_HEADS * B_DK, GROUP_W, GROUP_W, N_HEADS, N_HEADS,
    N_HEADS * C_D, N_HEADS * C_D, GROUP_W,
    N_HEADS * D_DK, N_HEADS * D_DK, GROUP_W, GROUP_W,
)
N_MIX_IN = sum(MIX_IN_SPLITS)

kernel_name = "hybrid_parallel_groups_hgrn2_gdn_stickbreak_retnet_macaron"


def _split_points(sizes):
    return [int(p) for p in np.cumsum(sizes)[:-1]]


def rmsnorm(x, g):
    xf = x.astype(jnp.float32)
    y = xf * lax.rsqrt(jnp.mean(xf * xf, axis=-1, keepdims=True) + EPS)
    return (y * g.astype(jnp.float32)).astype(x.dtype)


def head_rmsnorm(o, g):
    b, t, h, d = o.shape
    return rmsnorm(o, g.reshape(h, d)).reshape(b, t, h * d)


def l2norm(x):
    return x * lax.rsqrt(jnp.sum(x * x, axis=-1, keepdims=True) + EPS)


def swiglu(h, w_in, w_out):
    gate, up = jnp.split(h @ w_in, 2, axis=-1)
    return (jax.nn.silu(gate) * up) @ w_out


def to_chunks(x):
    b, t = x.shape[:2]
    y = x.reshape((b, t // CHUNK, CHUNK) + x.shape[2:])
    return jnp.moveaxis(y, 3, 1)


def from_chunks(y):
    b, h, n, c, d = y.shape
    return jnp.moveaxis(y, 1, 3).reshape(b, n * c, h, d)


def causal_conv(x, w):
    t = x.shape[1]
    xp = jnp.pad(x, ((0, 0), (CONV_W - 1, 0), (0, 0)))
    y = xp[:, 0:t] * w[0]
    for j in range(1, CONV_W):
        y = y + xp[:, j:j + t] * w[j]
    return y


def rotary(x, pos):
    half = x.shape[-1] // 2
    inv_freq = ROPE_BASE ** (-jnp.arange(half, dtype=jnp.float32) / half)
    ang = pos.astype(jnp.float32)[:, None] * inv_freq[None, :]
    cos = jnp.cos(ang)[None, :, None, :]
    sin = jnp.sin(ang)[None, :, None, :]
    x1, x2 = x[..., :half], x[..., half:]
    return jnp.concatenate([x1 * cos - x2 * sin, x1 * sin + x2 * cos], axis=-1)


def hgrn2_mixer(q, f_logit, i, g, lb, norm_g):
    dtype = q.dtype
    b, t, _ = q.shape
    q, f_logit, i, g, lb = (a.astype(jnp.float32) for a in (q, f_logit, i, g, lb))
    f = lb + (1.0 - lb) * jax.nn.sigmoid(f_logit)
    qc = to_chunks(jax.nn.silu(q).reshape(b, t, N_HEADS, A_DK))
    kc = to_chunks((1.0 - f).reshape(b, t, N_HEADS, A_DK))
    vc = to_chunks(jax.nn.silu(i).reshape(b, t, N_HEADS, A_DV))
    bc = jnp.cumsum(to_chunks(jnp.log(f).reshape(b, t, N_HEADS, A_DK)), axis=3)
    causal = jnp.tril(jnp.ones((CHUNK, CHUNK), dtype=bool))[:, :, None]

    def step(state, inp):
        q_, k_, v_, b_ = inp
        rel = jnp.where(causal, b_[:, :, :, None, :] - b_[:, :, None, :, :], -jnp.inf)
        scores = jnp.einsum('bhtk,bhsk,bhtsk->bhts', q_, k_, jnp.exp(rel))
        o = (jnp.einsum('bhts,bhsv->bhtv', scores, v_)
             + jnp.einsum('bhtk,bhkv->bhtv', q_ * jnp.exp(b_), state))
        b_last = b_[:, :, -1:, :]
        state = (jnp.exp(b_last[:, :, 0, :, None]) * state
                 + jnp.einsum('bhsk,bhsv->bhkv', k_ * jnp.exp(b_last - b_), v_))
        return state, o

    xs = tuple(jnp.moveaxis(a, 2, 0) for a in (qc, kc, vc, bc))
    s0 = jnp.zeros((b, N_HEADS, A_DK, A_DV), jnp.float32)
    _, o = lax.scan(step, s0, xs)
    o = from_chunks(jnp.moveaxis(o, 0, 2))
    return (head_rmsnorm(o, norm_g) * jax.nn.sigmoid(g)).astype(dtype)


def gated_deltanet_mixer(q, k, v, g, a, beta_logit, conv_w, a_log, dt_bias, norm_g):
    dtype = q.dtype
    b, t, _ = q.shape
    qkv = jnp.concatenate([q, k, v], axis=-1).astype(jnp.float32)
    qkv = jax.nn.silu(causal_conv(qkv, conv_w.astype(jnp.float32)))
    q, k, v = jnp.split(qkv, [N_HEADS * B_DK, 2 * N_HEADS * B_DK], axis=-1)
    q = l2norm(q.reshape(b, t, N_HEADS, B_DK)) * B_DK ** -0.5
    k = l2norm(k.reshape(b, t, N_HEADS, B_DK))
    v = v.reshape(b, t, N_HEADS, B_DV)
    beta = jax.nn.sigmoid(beta_logit.astype(jnp.float32))
    log_alpha = -jnp.exp(a_log.astype(jnp.float32)) * jax.nn.softplus(
        a.astype(jnp.float32) + dt_bias.astype(jnp.float32))
    qc, kc, vc = to_chunks(q), to_chunks(k), to_chunks(v)
    betac = to_chunks(beta)[..., None]
    gc = jnp.cumsum(to_chunks(log_alpha), axis=3)
    tril = jnp.tril(jnp.ones((CHUNK, CHUNK), dtype=bool))
    strict = jnp.tril(jnp.ones((CHUNK, CHUNK), dtype=bool), -1)
    decay = jnp.exp(jnp.where(tril, gc[..., :, None] - gc[..., None, :], -jnp.inf))
    k_beta = kc * betac
    m = jnp.where(strict, jnp.einsum('bhnik,bhnjk->bhnij', k_beta, kc) * decay, 0.0)
    rhs = jnp.concatenate([vc * betac, k_beta * jnp.exp(gc)[..., None]], axis=-1)
    sol = lax.linalg.triangular_solve(m + jnp.eye(CHUNK, dtype=jnp.float32), rhs,
                                      left_side=True, lower=True, unit_diagonal=True)
    u, w = jnp.split(sol, [B_DV], axis=-1)
    attn = jnp.where(tril, jnp.einsum('bhnik,bhnjk->bhnij', qc, kc) * decay, 0.0)

    def step(state, inp):
        q_, k_, u_, w_, g_, a_ = inp
        v_new = u_ - jnp.einsum('bhck,bhkv->bhcv', w_, state)
        o = (jnp.einsum('bhck,bhkv->bhcv', q_ * jnp.exp(g_)[..., None], state)
             + jnp.einsum('bhij,bhjv->bhiv', a_, v_new))
        g_last = g_[..., -1:]
        state = (jnp.exp(g_last)[..., None] * state
                 + jnp.einsum('bhck,bhcv->bhkv', k_ * jnp.exp(g_last - g_)[..., None], v_new))
        return state, o

    xs = tuple(jnp.moveaxis(z, 2, 0) for z in (qc, kc, u, w, gc, attn))
    s0 = jnp.zeros((b, N_HEADS, B_DK, B_DV), jnp.float32)
    _, o = lax.scan(step, s0, xs)
    o = from_chunks(jnp.moveaxis(o, 0, 2))
    return (head_rmsnorm(o, norm_g) * jax.nn.silu(g.astype(jnp.float32))).astype(dtype)


def stick_breaking_mixer(q, k, v, norm_g):
    dtype = q.dtype
    b, t, _ = q.shape

    def heads(z):
        return z.astype(jnp.float32).reshape(b, t, N_HEADS, C_D).transpose(0, 2, 1, 3)

    qh, kh, vh = heads(q) * C_D ** -0.5, heads(k), heads(v)
    blocks = []
    for blk in range(t // Q_BLOCK):
        t0, t1 = blk * Q_BLOCK, (blk + 1) * Q_BLOCK
        z = jnp.einsum('bhtd,bhsd->bhts', qh[:, :, t0:t1], kh[:, :, :t1])
        past = jnp.arange(t0, t1)[:, None] > jnp.arange(t1)[None, :]
        log_stay = jnp.where(past, jax.nn.log_sigmoid(-z), 0.0)
        between = lax.cumsum(log_stay, axis=3, reverse=True) - log_stay
        wts = jnp.where(past, jnp.exp(jax.nn.log_sigmoid(z) + between), 0.0)
        blocks.append(jnp.einsum('bhts,bhsd->bhtd', wts, vh[:, :, :t1]))
    o = jnp.concatenate(blocks, axis=2).transpose(0, 2, 1, 3)
    return head_rmsnorm(o, norm_g).astype(dtype)


def retention_mixer(q, k, v, g, norm_g):
    dtype = q.dtype
    b, t, _ = q.shape
    pos = jnp.arange(t)
    q = rotary(q.astype(jnp.float32).reshape(b, t, N_HEADS, D_DK), pos)
    k = rotary(k.astype(jnp.float32).reshape(b, t, N_HEADS, D_DK), pos) * D_DK ** -0.5
    v = v.astype(jnp.float32).reshape(b, t, N_HEADS, D_DV)
    log_gamma = jnp.log(1.0 - 2.0 ** (-5.0 - jnp.arange(N_HEADS, dtype=jnp.float32)))
    c = jnp.arange(CHUNK, dtype=jnp.float32)
    rel = c[:, None] - c[None, :]
    intra_decay = jnp.where(rel[None] >= 0, jnp.exp(rel[None] * log_gamma[:, None, None]), 0.0)
    qc, kc, vc = to_chunks(q), to_chunks(k), to_chunks(v)
    scores = jnp.einsum('bhntk,bhnsk->bhnts', qc, kc) * intra_decay[None, :, None]
    o_intra = jnp.einsum('bhnts,bhnsv->bhntv', scores, vc)
    zeta = jnp.exp((CHUNK - 1 - c)[None, :] * log_gamma[:, None])
    xi = jnp.exp((c + 1.0)[None, :] * log_gamma[:, None])
    chunk_kv = jnp.einsum('bhnsk,bhnsv,hs->nbhkv', kc, vc, zeta)
    gamma_c = jnp.exp(CHUNK * log_gamma)[:, None, None]

    def step(state, kv):
        return gamma_c * state + kv, state

    s0 = jnp.zeros((b, N_HEADS, D_DK, D_DV), jnp.float32)
    _, prev = lax.scan(step, s0, chunk_kv)
    o_cross = jnp.einsum('bhntk,nbhkv,ht->bhntv', qc, prev, xi)
    o = from_chunks(o_intra + o_cross)
    return (head_rmsnorm(o, norm_g) * jax.nn.silu(g.astype(jnp.float32))).astype(dtype)


def setup_inputs(seed: int = 0) -> dict:
    key = jax.random.key(seed)
    ks = jax.random.split(key, 20)
    f32 = jnp.float32

    def normal(k, shape, scale):
        return jax.random.normal(k, shape, f32) * scale

    def gain(k, shape):
        return 1.0 + 0.02 * jax.random.normal(k, shape, f32)

    dt = jnp.exp(jax.random.uniform(ks[14], (DEPTH, N_HEADS), f32, math.log(1e-3), math.log(1e-1)))
    dt_bias = dt + jnp.log(-jnp.expm1(-dt))
    return {
        "x": normal(ks[0], (BATCH, SEQ, D_MODEL), 1.0),
        "ffn1_norm": gain(ks[1], (DEPTH, D_MODEL)),
        "ffn1_w_in": normal(ks[2], (DEPTH, D_MODEL, 2 * D_FF), D_MODEL ** -0.5),
        "ffn1_w_out": normal(ks[3], (DEPTH, D_FF, D_MODEL), D_FF ** -0.5),
        "mix_norm": gain(ks[4], (DEPTH, D_MODEL)),
        "mix_w_in": normal(ks[5], (DEPTH, D_MODEL, N_MIX_IN), D_MODEL ** -0.5),
        "mix_w_out": normal(ks[6], (DEPTH, D_MIX, D_MODEL), D_MIX ** -0.5),
        "ffn2_norm": gain(ks[7], (DEPTH, D_MODEL)),
        "ffn2_w_in": normal(ks[8], (DEPTH, D_MODEL, 2 * D_FF), D_MODEL ** -0.5),
        "ffn2_w_out": normal(ks[9], (DEPTH, D_FF, D_MODEL), D_FF ** -0.5),
        "hgrn_lb_logits": normal(ks[10], (DEPTH, N_HEADS * A_DK), 0.5),
        "hgrn_out_norm": gain(ks[11], (DEPTH, GROUP_W)),
        "gdn_conv_w": normal(ks[12], (DEPTH, CONV_W, CONV_CH), CONV_W ** -0.5),
        "gdn_a_log": jnp.log(jax.random.uniform(ks[13], (DEPTH, N_HEADS), f32, 1.0, 16.0)),
        "gdn_dt_bias": dt_bias,
        "gdn_out_norm": gain(ks[15], (DEPTH, GROUP_W)),
        "sb_out_norm": gain(ks[16], (DEPTH, GROUP_W)),
        "ret_out_norm": gain(ks[17], (DEPTH, GROUP_W)),
        "final_norm": gain(ks[18], (D_MODEL,)),
    }


def reference(x, ffn1_norm, ffn1_w_in, ffn1_w_out, mix_norm, mix_w_in, mix_w_out,
              ffn2_norm, ffn2_w_in, ffn2_w_out, hgrn_lb_logits, hgrn_out_norm,
              gdn_conv_w, gdn_a_log, gdn_dt_bias, gdn_out_norm, sb_out_norm,
              ret_out_norm, final_norm):
    lb_all = jnp.cumsum(jax.nn.softmax(hgrn_lb_logits.astype(jnp.float32), axis=0), axis=0)
    lb_all = lb_all - lb_all[0]
    points = _split_points(MIX_IN_SPLITS)
    for l in range(DEPTH):
        x = x + 0.5 * swiglu(rmsnorm(x, ffn1_norm[l]), ffn1_w_in[l], ffn1_w_out[l])
        h = rmsnorm(x, mix_norm[l])
        (a_q, a_f, a_i, a_g,
         b_q, b_k, b_v, b_g, b_a, b_b,
         c_q, c_k, c_v,
         d_q, d_k, d_v, d_g) = jnp.split(h @ mix_w_in[l], points, axis=-1)
        y_a = hgrn2_mixer(a_q, a_f, a_i, a_g, lb_all[l], hgrn_out_norm[l])
        y_b = gated_deltanet_mixer(b_q, b_k, b_v, b_g, b_a, b_b, gdn_conv_w[l],
                                   gdn_a_log[l], gdn_dt_bias[l], gdn_out_norm[l])
        y_c = stick_breaking_mixer(c_q, c_k, c_v, sb_out_norm[l])
        y_d = retention_mixer(d_q, d_k, d_v, d_g, ret_out_norm[l])
        y = jnp.concatenate([y_a, y_b, y_c, y_d], axis=-1)
        x = x + y @ mix_w_out[l]
        x = x + 0.5 * swiglu(rmsnorm(x, ffn2_norm[l]), ffn2_w_in[l], ffn2_w_out[l])
    return rmsnorm(x, final_norm)
```

```python
import functools
import math

import numpy as np
import jax
import jax.numpy as jnp
from jax import lax
from jax.experimental import pallas as pl
from jax.experimental.pallas import tpu as pltpu

F32 = jnp.float32
BF16 = jnp.bfloat16

EPS = 1e-6
CHUNK = 64
N_HEADS = 4
HEAD_W = 64
GROUP_W = N_HEADS * HEAD_W
D_DK = 32
ROPE_BASE = 10000.0
N_LEVELS = 6
VMEM_LIMIT = 56 * 1024 * 1024

MIXER_ROWS = 256
SB_BLOCK = 128


def _bf(x):
    return x.astype(BF16)


def _dot(a, b):
    return jnp.dot(a, b, preferred_element_type=F32)


def _dot_nt(a, b):
    return lax.dot_general(a, b, (((1,), (1,)), ((), ())), preferred_element_type=F32)


def _dot_tn(a, b):
    return lax.dot_general(a, b, (((0,), (0,)), ((), ())), preferred_element_type=F32)


def _split2(x):
    hi = _bf(x)
    lo = _bf(x - hi.astype(F32))
    return hi, lo


def _split3(x):
    h1 = _bf(x)
    r = x - h1.astype(F32)
    h2 = _bf(r)
    h3 = _bf(r - h2.astype(F32))
    return h1, h2, h3


def _select_rows(w, x):
    h1, h2, h3 = _split3(x)
    return _dot(w, h1) + _dot(w, h2) + _dot(w, h3)


def _headsum(x, gmat):
    hi, lo = _split2(x)
    return _dot(hi, gmat) + _dot(lo, gmat)


def _head_rmsnorm(o, gmat, gain):
    ms = _headsum(o * o, gmat) * (1.0 / HEAD_W)
    return o * lax.rsqrt(ms + EPS) * gain


def _sigmoid(x):
    return 1.0 / (1.0 + jnp.exp(-x))


def _silu(x):
    return x * _sigmoid(x)


def _softplus(x):
    return jnp.maximum(x, 0.0) + jnp.log(1.0 + jnp.exp(-jnp.abs(x)))


def _mm_split(a, b, passes):
    if passes == 1:
        return _dot(_bf(a), _bf(b))
    ah, al = _split2(a)
    bh, bl = _split2(b)
    return _dot(ah, bh) + _dot(ah, bl) + _dot(al, bh)


def _head_masks(width, lanes_per_head, dtype):
    lane = lax.broadcasted_iota(jnp.int32, (1, width), 1)
    return [((lane // lanes_per_head) == h).astype(dtype) for h in range(N_HEADS)]


def _level_masks(width):
    r = lax.broadcasted_iota(jnp.int32, (CHUNK, CHUNK), 0)
    c = lax.broadcasted_iota(jnp.int32, (CHUNK, CHUNK), 1)
    rows = lax.broadcasted_iota(jnp.int32, (CHUNK, width), 0)
    pair, upper, lower = [], [], []
    for l in range(N_LEVELS):
        n = CHUNK >> l
        m = n // 2
        sh = N_LEVELS - l
        same = (r >> sh) == (c >> sh)
        pair.append((same & ((r & (n - 1)) >= m) & ((c & (n - 1)) < m)).astype(F32))
        upper.append((rows & (n - 1)) >= m)
        lower.append((rows & (n - 1)) < m)
    eye = (r == c).astype(F32)
    return pair, upper, lower, eye


def _decay_factors(cs, upper, lower):
    b = cs[0:CHUNK]
    out = []
    for l in range(N_LEVELS):
        ab = cs[(2 + l) * CHUNK:(3 + l) * CHUNK]
        fq = jnp.exp(jnp.where(upper[l], b - ab, 0.0))
        fk = jnp.exp(jnp.where(lower[l], ab - b, 0.0))
        out.append((fq, fk))
    return out


def _cumsum_table():
    t = np.arange(CHUNK)
    tri = (t[None, :] <= t[:, None]).astype(np.float32)
    up = (t[None, :] > t[:, None]).astype(np.float32)
    blocks = [tri, up]
    for l in range(N_LEVELS):
        n = CHUNK >> l
        anchor = (t // n) * n + n // 2 - 1
        blocks.append(tri[anchor])
    return jnp.asarray(np.concatenate(blocks, axis=0), dtype=BF16)


def _head_block_matrix(rows, rows_per_head, cols, cols_per_head, dtype):
    r = np.arange(rows)[:, None] // rows_per_head
    c = np.arange(cols)[None, :] // cols_per_head
    return jnp.asarray((r == c).astype(np.float32), dtype=dtype)


def _ret_lane_head(p):
    return (p % 64) // 16


def _retention_tables(seq, chunk):
    p = np.arange(128)
    half = D_DK // 2
    inv_freq = ROPE_BASE ** (-(p % 16).astype(np.float64) / half)
    ang = np.arange(seq, dtype=np.float64)[:, None] * inv_freq[None, :]
    cos = np.cos(ang)
    sin = np.sin(ang) * np.where(p < 64, -1.0, 1.0)[None, :]
    log_gamma = np.log(1.0 - 2.0 ** (-5.0 - np.arange(N_HEADS, dtype=np.float64)))
    c = np.arange(chunk, dtype=np.float64)
    rel = c[:, None] - c[None, :]
    decay = np.where(rel[None] >= 0, np.exp(rel[None] * log_gamma[:, None, None]), 0.0)
    lane_h = _ret_lane_head(p)
    zeta = np.exp((chunk - 1 - c)[:, None] * log_gamma[lane_h][None, :])
    xi = np.exp((c + 1.0)[:, None] * log_gamma[lane_h][None, :])
    gamma_c = np.exp(chunk * log_gamma)[np.arange(GROUP_W) // HEAD_W][None, :]
    bd = (lane_h[:, None] == (np.arange(GROUP_W) // HEAD_W)[None, :]).astype(np.float32)
    f = lambda a: jnp.asarray(a, dtype=F32)
    return f(cos), f(sin), f(decay), f(zeta), f(xi), f(gamma_c), f(bd)


def _ret_perm():
    p = np.arange(128)
    h = _ret_lane_head(p)
    return h * D_DK + (p % 16) + np.where(p >= 64, 16, 0)


def _rms_rows(x, gain):
    return x * lax.rsqrt(jnp.mean(x * x, axis=-1, keepdims=True) + EPS) * gain


def _ffn_body(x_ref, g_ref, wg_ref, wu_ref, wo_ref, o_ref, h_scr, acc_scr):
    j = pl.program_id(1)

    @pl.when(j == 0)
    def _():
        h_scr[...] = _bf(_rms_rows(x_ref[...], g_ref[...]))
        acc_scr[...] = jnp.zeros_like(acc_scr)

    h = h_scr[...]
    a = _dot(h, wg_ref[...])
    b = _dot(h, wu_ref[...])
    acc_scr[...] += _dot(_bf(_silu(a) * b), wo_ref[...])

    @pl.when(j == pl.num_programs(1) - 1)
    def _():
        o_ref[...] = x_ref[...] + 0.5 * acc_scr[...]


def _ffn(x, gain, w_in, w_out, *, tm, tf):
    n, d = x.shape
    d_ff = w_out.shape[0]
    nf = d_ff // tf
    return pl.pallas_call(
        _ffn_body,
        grid=(n // tm, nf),
        in_specs=[
            pl.BlockSpec((tm, d), lambda i, j: (i, 0)),
            pl.BlockSpec((1, d), lambda i, j: (0, 0)),
            pl.BlockSpec((d, tf), lambda i, j: (0, j)),
            pl.BlockSpec((d, tf), lambda i, j: (0, nf + j)),
            pl.BlockSpec((tf, d), lambda i, j: (j, 0)),
        ],
        out_specs=pl.BlockSpec((tm, d), lambda i, j: (i, 0)),
        out_shape=jax.ShapeDtypeStruct((n, d), F32),
        scratch_shapes=[pltpu.VMEM((tm, d), BF16), pltpu.VMEM((tm, d), F32)],
        compiler_params=pltpu.CompilerParams(
            dimension_semantics=("parallel", "arbitrary"), vmem_limit_bytes=VMEM_LIMIT),
        name="ffn",
    )(x, gain, w_in, w_in, w_out)


def _proj_body(x_ref, g_ref, wa_ref, wb_ref, wc_ref, wd_ref, oa_ref, ob_ref, oc_ref, od_ref):
    h = _bf(_rms_rows(x_ref[...], g_ref[...]))
    oa_ref[...] = _dot(h, wa_ref[...])
    ob_ref[...] = _dot(h, wb_ref[...])
    oc_ref[...] = _bf(_dot(h, wc_ref[...]))
    od_ref[...] = _dot(h, wd_ref[...])


def _mix_proj(x, gain, wa, wb, wc, wd, *, tm):
    n, d = x.shape
    ws = (wa, wb, wc, wd)
    dts = (F32, F32, BF16, F32)
    return pl.pallas_call(
        _proj_body,
        grid=(n // tm,),
        in_specs=[pl.BlockSpec((tm, d), lambda i: (i, 0)), pl.BlockSpec((1, d), lambda i: (0, 0))]
        + [pl.BlockSpec(w.shape, lambda i: (0, 0)) for w in ws],
        out_specs=[pl.BlockSpec((tm, w.shape[1]), lambda i: (i, 0)) for w in ws],
        out_shape=[jax.ShapeDtypeStruct((n, w.shape[1]), dt) for w, dt in zip(ws, dts)],
        compiler_params=pltpu.CompilerParams(
            dimension_semantics=("parallel",), vmem_limit_bytes=VMEM_LIMIT),
        name="mix_proj",
    )(x, gain, *ws)


def _mix_out_body(x_ref, ya_ref, yb_ref, yc_ref, yd_ref, w_ref, o_ref):
    acc = x_ref[...]
    for m, y_ref in enumerate((ya_ref, yb_ref, yc_ref, yd_ref)):
        acc = acc + _dot(y_ref[...], w_ref[m * GROUP_W:(m + 1) * GROUP_W, :])
    o_ref[...] = acc


def _mix_out(x, ys, w, *, tm):
    n, d = x.shape
    return pl.pallas_call(
        _mix_out_body,
        grid=(n // tm,),
        in_specs=[pl.BlockSpec((tm, d), lambda i: (i, 0))]
        + [pl.BlockSpec((tm, GROUP_W), lambda i: (i, 0)) for _ in ys]
        + [pl.BlockSpec(w.shape, lambda i: (0, 0))],
        out_specs=pl.BlockSpec((tm, d), lambda i: (i, 0)),
        out_shape=jax.ShapeDtypeStruct((n, d), F32),
        compiler_params=pltpu.CompilerParams(
            dimension_semantics=("parallel",), vmem_limit_bytes=VMEM_LIMIT),
        name="mix_out",
    )(x, *ys, w)


def _final_norm_body(x_ref, g_ref, o_ref):
    o_ref[...] = _rms_rows(x_ref[...], g_ref[...])


def _final_norm(x, gain, *, tm):
    n, d = x.shape
    return pl.pallas_call(
        _final_norm_body,
        grid=(n // tm,),
        in_specs=[pl.BlockSpec((tm, d), lambda i: (i, 0)), pl.BlockSpec((1, d), lambda i: (0, 0))],
        out_specs=pl.BlockSpec((tm, d), lambda i: (i, 0)),
        out_shape=jax.ShapeDtypeStruct((n, d), F32),
        compiler_params=pltpu.CompilerParams(dimension_semantics=("parallel",)),
        name="final_norm",
    )(x, gain)


def _hgrn_body(layer, q_ref, f_ref, i_ref, g_ref, lbl_ref, ng_ref, ctab_ref, gmat_ref, bd_ref,
               o_ref, st_scr):
    @pl.when(pl.program_id(1) == 0)
    def _():
        st_scr[...] = jnp.zeros_like(st_scr)

    logits = lbl_ref[...]
    e = jnp.exp(logits - jnp.max(logits, axis=0, keepdims=True))
    p = e / jnp.sum(e, axis=0, keepdims=True)
    lb = jnp.zeros_like(p[0:1])
    for r in range(1, layer + 1):
        lb = lb + p[r:r + 1]

    pair, upper, lower, eye = _level_masks(GROUP_W)
    hm = _head_masks(GROUP_W, HEAD_W, BF16)
    ctab = ctab_ref[...]
    gmat = gmat_ref[...]
    bd = bd_ref[...]
    ng = ng_ref[...]

    for c in range(q_ref.shape[1] // CHUNK):
        sl = pl.ds(c * CHUNK, CHUNK)
        q = q_ref[0, sl, :]
        f = lb + (1.0 - lb) * _sigmoid(f_ref[0, sl, :])
        qs = _silu(q)
        kk = 1.0 - f
        v = _bf(_silu(i_ref[0, sl, :]))
        cs = _select_rows(ctab, jnp.log(f))
        b = cs[0:CHUNK]
        rem = cs[CHUNK:2 * CHUNK]

        kk_b = _bf(kk)
        qs_b = _bf(qs)
        scores = [eye * _dot_nt(qs_b * hm[h], kk_b) for h in range(N_HEADS)]
        for l, (fq, fk) in enumerate(_decay_factors(cs, upper, lower)):
            ql = _bf(qs * fq)
            kl = _bf(kk * fk)
            for h in range(N_HEADS):
                scores[h] = scores[h] + pair[l] * _dot_nt(ql * hm[h], kl)

        st = st_scr[...]
        o = _dot_nt(_bf(qs * jnp.exp(b)), _bf(st))
        for h in range(N_HEADS):
            o = o + _dot(_bf(scores[h]), v * hm[h])

        kd = _bf(kk * jnp.exp(rem))
        st_scr[...] = st * jnp.exp(b[CHUNK - 1:CHUNK]) + _dot_tn(v, kd) * bd

        y = _head_rmsnorm(o, gmat, ng) * _sigmoid(g_ref[0, sl, :])
        o_ref[0, sl, :] = _bf(y)


def _hgrn_mixer(proj_a, lb_logits, norm_g, ctab, gmat, bd, layer, rows):
    bsz, seq, _ = proj_a.shape
    col = lambda c: pl.BlockSpec((1, rows, GROUP_W), lambda b, t, c=c: (b, t, c))
    whole = lambda a: pl.BlockSpec(a.shape, lambda b, t: (0,) * a.ndim)
    return pl.pallas_call(
        functools.partial(_hgrn_body, layer),
        grid=(bsz, seq // rows),
        in_specs=[col(0), col(1), col(2), col(3), whole(lb_logits), whole(norm_g),
                  whole(ctab), whole(gmat), whole(bd)],
        out_specs=pl.BlockSpec((1, rows, GROUP_W), lambda b, t: (b, t, 0)),
        out_shape=jax.ShapeDtypeStruct((bsz, seq, GROUP_W), BF16),
        scratch_shapes=[pltpu.VMEM((GROUP_W, GROUP_W), F32)],
        compiler_params=pltpu.CompilerParams(
            dimension_semantics=("parallel", "arbitrary"), vmem_limit_bytes=VMEM_LIMIT),
        name="hgrn2",
    )(proj_a, proj_a, proj_a, proj_a, lb_logits, norm_g, ctab, gmat, bd)


GDN_SOLVE_PASSES = 3


def _gdn_body(q_ref, k_ref, v_ref, g_ref, a_ref, bl_ref, cw_ref, alog_ref, dtb_ref, ng_ref,
              ctab_ref, gmat_ref, bd_ref, o_ref, s_scr, tail_scr, ext_scr):
    @pl.when(pl.program_id(1) == 0)
    def _():
        s_scr[...] = jnp.zeros_like(s_scr)
        tail_scr[...] = jnp.zeros_like(tail_scr)

    pair, upper, lower, eye = _level_masks(GROUP_W)
    hm = _head_masks(GROUP_W, HEAD_W, BF16)
    hmf = _head_masks(GROUP_W, HEAD_W, F32)
    ctab = ctab_ref[...]
    gmat = gmat_ref[...]
    bd = bd_ref[...]
    ng = ng_ref[...]
    neg_a = -jnp.exp(alog_ref[...])
    dtb = dtb_ref[...]
    raw = (q_ref, k_ref, v_ref)
    rows = q_ref.shape[1]

    for j in range(3):
        ext_scr[0:8, j * GROUP_W:(j + 1) * GROUP_W] = tail_scr[:, j * GROUP_W:(j + 1) * GROUP_W]
        ext_scr[8:8 + CHUNK, j * GROUP_W:(j + 1) * GROUP_W] = raw[j][0, 0:CHUNK, :]
        tail_scr[:, j * GROUP_W:(j + 1) * GROUP_W] = raw[j][0, rows - 8:rows, :]

    def conv(c, j):
        w = cw_ref[:, j * GROUP_W:(j + 1) * GROUP_W]
        acc = None
        for d in range(4):
            if c == 0:
                x = ext_scr[pl.ds(8 - d, CHUNK), j * GROUP_W:(j + 1) * GROUP_W]
            else:
                x = raw[j][0, pl.ds(c * CHUNK - d, CHUNK), :]
            term = x * w[3 - d:4 - d, :]
            acc = term if acc is None else acc + term
        return _silu(acc)

    for c in range(rows // CHUNK):
        sl = pl.ds(c * CHUNK, CHUNK)
        q = conv(c, 0)
        k = conv(c, 1)
        v = conv(c, 2)
        q = q * lax.rsqrt(_headsum(q * q, gmat) + EPS) * (HEAD_W ** -0.5)
        k = k * lax.rsqrt(_headsum(k * k, gmat) + EPS)
        beta = _sigmoid(bl_ref[0, sl, :])
        log_alpha = neg_a * _softplus(a_ref[0, sl, :] + dtb)
        cs = _select_rows(ctab, log_alpha)
        gc = cs[0:CHUNK]
        rem = cs[CHUNK:2 * CHUNK]
        kb = k * beta

        q_b = _bf(q)
        k_b = _bf(k)
        attn = [eye * _dot_nt(q_b * hm[h], k_b) for h in range(N_HEADS)]
        m_lev = [[None] * N_LEVELS for _ in range(N_HEADS)]
        for l, (fq, fk) in enumerate(_decay_factors(cs, upper, lower)):
            ql = _bf(q * fq)
            kbl = _bf(kb * fq)
            kl = _bf(k * fk)
            for h in range(N_HEADS):
                attn[h] = attn[h] + pair[l] * _dot_nt(ql * hm[h], kl)
                m_lev[h][l] = pair[l] * _dot_nt(kbl * hm[h], kl)

        vb = v * beta
        kbg = kb * jnp.exp(gc)
        u = jnp.zeros((CHUNK, GROUP_W), F32)
        w = jnp.zeros((CHUNK, GROUP_W), F32)
        for h in range(N_HEADS):
            t_inv = eye - m_lev[h][N_LEVELS - 1]
            for l in range(N_LEVELS - 2, -1, -1):
                tl = _mm_split(t_inv, m_lev[h][l], GDN_SOLVE_PASSES)
                t_inv = t_inv - _mm_split(tl, t_inv, GDN_SOLVE_PASSES)
            th, tlo = _split2(t_inv)
            vbh = _bf(vb * hmf[h])
            kbgh = _bf(kbg * hmf[h])
            u = u + _dot(th, vbh) + _dot(tlo, vbh)
            w = w + _dot(th, kbgh) + _dot(tlo, kbgh)

        s = s_scr[...]
        s_b = _bf(s)
        v_new = u - _dot(_bf(w), s_b)
        o = _dot(_bf(q * jnp.exp(gc)), s_b)
        v_new_b = _bf(v_new)
        for h in range(N_HEADS):
            o = o + _dot(_bf(attn[h]), v_new_b * hm[h])
        kd = _bf(k * jnp.exp(rem))
        s_scr[...] = s * jnp.exp(gc[CHUNK - 1:CHUNK]) + _dot_tn(kd, v_new_b) * bd

        y = _head_rmsnorm(o, gmat, ng) * _silu(g_ref[0, sl, :])
        o_ref[0, sl, :] = _bf(y)


def _gdn_mixer(proj_b, conv_w, a_log_rep, dt_bias_rep, norm_g, ctab, gmat, bd, rows):
    bsz, seq, _ = proj_b.shape
    col = lambda c: pl.BlockSpec((1, rows, GROUP_W), lambda b, t, c=c: (b, t, c))
    whole = lambda a: pl.BlockSpec(a.shape, lambda b, t: (0,) * a.ndim)
    return pl.pallas_call(
        _gdn_body,
        grid=(bsz, seq // rows),
        in_specs=[col(0), col(1), col(2), col(3), col(4), col(5), whole(conv_w), whole(a_log_rep),
                  whole(dt_bias_rep), whole(norm_g), whole(ctab), whole(gmat), whole(bd)],
        out_specs=pl.BlockSpec((1, rows, GROUP_W), lambda b, t: (b, t, 0)),
        out_shape=jax.ShapeDtypeStruct((bsz, seq, GROUP_W), BF16),
        scratch_shapes=[pltpu.VMEM((GROUP_W, GROUP_W), F32),
                        pltpu.VMEM((8, 3 * GROUP_W), F32),
                        pltpu.VMEM((8 + CHUNK, 3 * GROUP_W), F32)],
        compiler_params=pltpu.CompilerParams(
            dimension_semantics=("parallel", "arbitrary"), vmem_limit_bytes=VMEM_LIMIT),
        name="gdn",
    )(proj_b, proj_b, proj_b, proj_b, proj_b, proj_b, conv_w, a_log_rep, dt_bias_rep, norm_g,
      ctab, gmat, bd)


def _sb_body(q_ref, k_ref, v_ref, ng_ref, u2_ref, gmat_ref, o_ref):
    qi = pl.program_id(1)
    blk = SB_BLOCK
    hm = _head_masks(GROUP_W, HEAD_W, BF16)
    u2 = u2_ref[...]
    q = q_ref[0] * (HEAD_W ** -0.5)
    qh = [q * hm[h] for h in range(N_HEADS)]
    r = lax.broadcasted_iota(jnp.int32, (blk, blk), 0)
    c = lax.broadcasted_iota(jnp.int32, (blk, blk), 1)
    past = r > c

    def block(kb, carry, acc, diagonal):
        kblk = k_ref[0, pl.ds(kb * blk, blk), :]
        vblk = v_ref[0, pl.ds(kb * blk, blk), :]
        new_carry = []
        for h in range(N_HEADS):
            z = _dot_nt(qh[h], kblk)
            l1p = jnp.log(1.0 + jnp.exp(-jnp.abs(z)))
            log_sig = jnp.minimum(z, 0.0) - l1p
            log_stay = -jnp.maximum(z, 0.0) - l1p
            if diagonal:
                log_stay = jnp.where(past, log_stay, 0.0)
            hi, lo = _split2(log_stay)
            sfx = _dot(hi, u2) + _dot(lo, u2)
            wts = jnp.exp(log_sig + sfx[:, :blk] + carry[h])
            if diagonal:
                wts = jnp.where(past, wts, 0.0)
            acc = acc + _dot(_bf(wts), vblk * hm[h])
            new_carry.append(carry[h] + sfx[:, blk:])
        return tuple(new_carry), acc

    zero = jnp.zeros((blk, blk), F32)
    carry, acc = block(qi, (zero,) * N_HEADS, jnp.zeros((blk, GROUP_W), F32), True)

    def step(i, state):
        return block(qi - 1 - i, state[0], state[1], False)

    carry, acc = lax.fori_loop(0, qi, step, (carry, acc))
    o_ref[0] = _bf(_head_rmsnorm(acc, gmat_ref[...], ng_ref[...]))


def _sb_mixer(proj_c, norm_g, u2, gmat):
    bsz, seq, _ = proj_c.shape
    whole = lambda a: pl.BlockSpec(a.shape, lambda b, t: (0,) * a.ndim)
    return pl.pallas_call(
        _sb_body,
        grid=(bsz, seq // SB_BLOCK),
        in_specs=[pl.BlockSpec((1, SB_BLOCK, GROUP_W), lambda b, t: (b, t, 0)),
                  pl.BlockSpec((1, seq, GROUP_W), lambda b, t: (b, 0, 1)),
                  pl.BlockSpec((1, seq, GROUP_W), lambda b, t: (b, 0, 2)),
                  whole(norm_g), whole(u2), whole(gmat)],
        out_specs=pl.BlockSpec((1, SB_BLOCK, GROUP_W), lambda b, t: (b, t, 0)),
        out_shape=jax.ShapeDtypeStruct((bsz, seq, GROUP_W), BF16),
        compiler_params=pltpu.CompilerParams(
            dimension_semantics=("parallel", "arbitrary"), vmem_limit_bytes=VMEM_LIMIT),
        name="stickbreak",
    )(proj_c, proj_c, proj_c, norm_g, u2, gmat)


def _sb_table():
    j = np.arange(SB_BLOCK)[:, None]
    s = np.arange(SB_BLOCK)[None, :]
    later = (j > s).astype(np.float32)
    return jnp.asarray(np.concatenate([later, np.ones_like(later)], axis=1), dtype=BF16)


def _ret_body(q_ref, k_ref, v_ref, g_ref, cos_ref, sin_ref, decay_ref, zeta_ref, xi_ref, gc_ref,
              bd_ref, ng_ref, gmat_ref, o_ref, s_scr):
    @pl.when(pl.program_id(1) == 0)
    def _():
        s_scr[...] = jnp.zeros_like(s_scr)

    chunk = decay_ref.shape[1]
    lane = lax.broadcasted_iota(jnp.int32, (1, 2 * HEAD_W), 1)
    hq = [(((lane % 64) // 16) == h).astype(BF16) for h in range(N_HEADS)]
    hv = _head_masks(GROUP_W, HEAD_W, BF16)
    gmat = gmat_ref[...]
    ng = ng_ref[...]
    bd = bd_ref[...]
    zeta = zeta_ref[...]
    xi = xi_ref[...]
    gamma_c = gc_ref[...]

    for c in range(q_ref.shape[1] // chunk):
        sl = pl.ds(c * chunk, chunk)
        cos = cos_ref[sl, :]
        sin = sin_ref[sl, :]
        q = q_ref[0, sl, :]
        k = k_ref[0, sl, :]
        qr = q * cos + pltpu.roll(q, 64, axis=1) * sin
        kr = (k * cos + pltpu.roll(k, 64, axis=1) * sin) * (D_DK ** -0.5)
        v = _bf(v_ref[0, sl, :])
        qr_b = _bf(qr)
        kr_b = _bf(kr)
        s = s_scr[...]
        o = _dot(_bf(qr * xi), _bf(s))
        for h in range(N_HEADS):
            sc = _dot_nt(qr_b * hq[h], kr_b) * decay_ref[h]
            o = o + _dot(_bf(sc), v * hv[h])
        s_scr[...] = s * gamma_c + _dot_tn(_bf(kr * zeta), v) * bd
        y = _head_rmsnorm(o, gmat, ng) * _silu(g_ref[0, sl, :])
        o_ref[0, sl, :] = _bf(y)


def _ret_mixer(proj_d, tables, norm_g, gmat, rows):
    bsz, seq, _ = proj_d.shape
    cos, sin, decay, zeta, xi, gamma_c, bd = tables
    whole = lambda a: pl.BlockSpec(a.shape, lambda b, t: (0,) * a.ndim)
    return pl.pallas_call(
        _ret_body,
        grid=(bsz, seq // rows),
        in_specs=[pl.BlockSpec((1, rows, 128), lambda b, t: (b, t, 0)),
                  pl.BlockSpec((1, rows, 128), lambda b, t: (b, t, 1)),
                  pl.BlockSpec((1, rows, GROUP_W), lambda b, t: (b, t, 1)),
                  pl.BlockSpec((1, rows, GROUP_W), lambda b, t: (b, t, 2)),
                  pl.BlockSpec((rows, 128), lambda b, t: (t, 0)),
                  pl.BlockSpec((rows, 128), lambda b, t: (t, 0)),
                  whole(decay), whole(zeta), whole(xi), whole(gamma_c), whole(bd),
                  whole(norm_g), whole(gmat)],
        out_specs=pl.BlockSpec((1, rows, GROUP_W), lambda b, t: (b, t, 0)),
        out_shape=jax.ShapeDtypeStruct((bsz, seq, GROUP_W), BF16),
        scratch_shapes=[pltpu.VMEM((2 * HEAD_W, GROUP_W), F32)],
        compiler_params=pltpu.CompilerParams(
            dimension_semantics=("parallel", "arbitrary"), vmem_limit_bytes=VMEM_LIMIT),
        name="retention",
    )(proj_d, proj_d, proj_d, proj_d, cos, sin, decay, zeta, xi, gamma_c, bd, norm_g, gmat)


def _mix_weights(w_in):
    a_w = w_in[:, 0:1024]
    b0 = 1024
    b_main = w_in[:, b0:b0 + 1024]
    b_a = jnp.repeat(w_in[:, b0 + 1024:b0 + 1028], HEAD_W, axis=1)
    b_b = jnp.repeat(w_in[:, b0 + 1028:b0 + 1032], HEAD_W, axis=1)
    c0 = b0 + 1032
    c_w = w_in[:, c0:c0 + 768]
    d0 = c0 + 768
    perm = _ret_perm()
    d_q = w_in[:, d0:d0 + 128][:, perm]
    d_k = w_in[:, d0 + 128:d0 + 256][:, perm]
    d_rest = w_in[:, d0 + 256:d0 + 768]
    b_w = jnp.concatenate([b_main, b_a, b_b], axis=1)
    d_w = jnp.concatenate([d_q, d_k, d_rest], axis=1)
    return _bf(a_w), _bf(b_w), _bf(c_w), _bf(d_w)


def kernel(x, ffn1_norm, ffn1_w_in, ffn1_w_out, mix_norm, mix_w_in, mix_w_out, ffn2_norm, ffn2_w_in,
           ffn2_w_out, hgrn_lb_logits, hgrn_out_norm, gdn_conv_w, gdn_a_log, gdn_dt_bias, gdn_out_norm,
           sb_out_norm, ret_out_norm, final_norm):
    bsz, seq, d = x.shape
    depth = ffn1_norm.shape[0]
    n = bsz * seq
    d_ff = ffn1_w_out.shape[1]
    tm_ffn, tf = 512, d_ff // 2
    tm = 256
    rows = MIXER_ROWS

    ctab = _cumsum_table()
    gmat = _head_block_matrix(GROUP_W, HEAD_W, GROUP_W, HEAD_W, BF16)
    bd = _head_block_matrix(GROUP_W, HEAD_W, GROUP_W, HEAD_W, F32)
    u2 = _sb_table()
    ret_tables = _retention_tables(seq, CHUNK)
    row = lambda a: a.reshape(1, -1).astype(F32)
    rep = lambda a: jnp.repeat(a.astype(F32), HEAD_W).reshape(1, GROUP_W)

    x = x.reshape(n, d)
    for l in range(depth):
        x = _ffn(x, row(ffn1_norm[l]), _bf(ffn1_w_in[l]), _bf(ffn1_w_out[l]), tm=tm_ffn, tf=tf)
        pa, pb, pc, pd = _mix_proj(x, row(mix_norm[l]), *_mix_weights(mix_w_in[l]), tm=tm)
        shp = lambda a: a.reshape(bsz, seq, a.shape[-1])
        ya = _hgrn_mixer(shp(pa), hgrn_lb_logits.astype(F32), row(hgrn_out_norm[l]), ctab, gmat, bd, l, rows)
        yb = _gdn_mixer(shp(pb), gdn_conv_w[l].astype(F32), rep(gdn_a_log[l]), rep(gdn_dt_bias[l]),
                        row(gdn_out_norm[l]), ctab, gmat, bd, rows)
        yc = _sb_mixer(shp(pc), row(sb_out_norm[l]), u2, gmat)
        yd = _ret_mixer(shp(pd), ret_tables, row(ret_out_norm[l]), gmat, rows)
        ys = [y.reshape(n, GROUP_W) for y in (ya, yb, yc, yd)]
        x = _mix_out(x, ys, _bf(mix_w_out[l]), tm=tm)
        x = _ffn(x, row(ffn2_norm[l]), _bf(ffn2_w_in[l]), _bf(ffn2_w_out[l]), tm=tm_ffn, tf=tf)
    return _final_norm(x, row(final_norm), tm=tm).reshape(bsz, seq, d)
```

```python
import functools
import math

import numpy as np
import jax
import jax.numpy as jnp
from jax import lax
from jax.experimental import pallas as pl
from jax.experimental.pallas import tpu as pltpu

F32 = jnp.float32
BF16 = jnp.bfloat16

EPS = 1e-6
CHUNK = 64
N_HEADS = 4
HEAD_W = 64
GROUP_W = N_HEADS * HEAD_W
D_DK = 32
ROPE_BASE = 10000.0
N_LEVELS = 6
VMEM_LIMIT = 56 * 1024 * 1024

MIXER_ROWS = 256
SB_BLOCK = 128
SB_UNIT = 256


def _bf(x):
    return x.astype(BF16)


def _dot(a, b):
    return jnp.dot(a, b, preferred_element_type=F32)


def _dot_nt(a, b):
    return lax.dot_general(a, b, (((1,), (1,)), ((), ())), preferred_element_type=F32)


def _dot_tn(a, b):
    return lax.dot_general(a, b, (((0,), (0,)), ((), ())), preferred_element_type=F32)


def _split2(x):
    hi = _bf(x)
    lo = _bf(x - hi.astype(F32))
    return hi, lo


def _split3(x):
    h1 = _bf(x)
    r = x - h1.astype(F32)
    h2 = _bf(r)
    h3 = _bf(r - h2.astype(F32))
    return h1, h2, h3


def _select_rows(w3, x):
    return _dot(w3, jnp.concatenate(_split3(x), axis=0))


def _headsum(x, gmat):
    hi, lo = _split2(x)
    return _dot(hi, gmat) + _dot(lo, gmat)


def _head_rmsnorm(o, gmat, gain):
    ms = _headsum(o * o, gmat) * (1.0 / HEAD_W)
    return o * lax.rsqrt(ms + EPS) * gain


def _sigmoid(x):
    return 1.0 / (1.0 + jnp.exp(-x))


def _silu(x):
    return x * _sigmoid(x)


def _softplus(x):
    return jnp.maximum(x, 0.0) + jnp.log(1.0 + jnp.exp(-jnp.abs(x)))


def _head_masks(width, lanes_per_head, dtype):
    lane = lax.broadcasted_iota(jnp.int32, (1, width), 1)
    return [((lane // lanes_per_head) == h).astype(dtype) for h in range(N_HEADS)]


def _stack_heads(x, masks):
    return jnp.concatenate([x * m for m in masks], axis=0)


def _level_masks(width, stack=1):
    r = lax.broadcasted_iota(jnp.int32, (stack * CHUNK, CHUNK), 0) & (CHUNK - 1)
    c = lax.broadcasted_iota(jnp.int32, (stack * CHUNK, CHUNK), 1)
    rows = lax.broadcasted_iota(jnp.int32, (CHUNK, width), 0)
    pair, upper, lower = [], [], []
    for l in range(N_LEVELS):
        n = CHUNK >> l
        m = n // 2
        sh = N_LEVELS - l
        same = (r >> sh) == (c >> sh)
        pair.append((same & ((r & (n - 1)) >= m) & ((c & (n - 1)) < m)).astype(F32))
        upper.append((rows & (n - 1)) >= m)
        lower.append((rows & (n - 1)) < m)
    eye = (r == c).astype(F32)
    return pair, upper, lower, eye


def _decay_factors(cs, upper, lower):
    b = cs[0:CHUNK]
    out = []
    for l in range(N_LEVELS):
        ab = cs[(2 + l) * CHUNK:(3 + l) * CHUNK]
        fq = jnp.exp(jnp.where(upper[l], b - ab, 0.0))
        fk = jnp.exp(jnp.where(lower[l], ab - b, 0.0))
        out.append((fq, fk))
    return out


def _cumsum_table():
    t = np.arange(CHUNK)
    tri = (t[None, :] <= t[:, None]).astype(np.float32)
    up = (t[None, :] > t[:, None]).astype(np.float32)
    blocks = [tri, up]
    for l in range(N_LEVELS):
        n = CHUNK >> l
        anchor = (t // n) * n + n // 2 - 1
        blocks.append(tri[anchor])
    tab = np.concatenate(blocks, axis=0)
    return jnp.asarray(np.concatenate([tab, tab, tab], axis=1), dtype=BF16)


def _head_block_matrix(rows, rows_per_head, cols, cols_per_head, dtype):
    r = np.arange(rows)[:, None] // rows_per_head
    c = np.arange(cols)[None, :] // cols_per_head
    return jnp.asarray((r == c).astype(np.float32), dtype=dtype)


def _ret_lane_head(p):
    return (p % 64) // 16


def _retention_tables(seq, chunk):
    p = np.arange(128)
    half = D_DK // 2
    inv_freq = ROPE_BASE ** (-(p % 16).astype(np.float64) / half)
    ang = np.arange(seq, dtype=np.float64)[:, None] * inv_freq[None, :]
    cos = np.cos(ang)
    sin = np.sin(ang) * np.where(p < 64, -1.0, 1.0)[None, :]
    log_gamma = np.log(1.0 - 2.0 ** (-5.0 - np.arange(N_HEADS, dtype=np.float64)))
    c = np.arange(chunk, dtype=np.float64)
    rel = c[:, None] - c[None, :]
    decay = np.where(rel[None] >= 0, np.exp(rel[None] * log_gamma[:, None, None]), 0.0)
    decay = decay.reshape(N_HEADS * chunk, chunk)
    lane_h = _ret_lane_head(p)
    zeta = np.exp((chunk - 1 - c)[:, None] * log_gamma[lane_h][None, :])
    xi = np.exp((c + 1.0)[:, None] * log_gamma[lane_h][None, :])
    gamma_c = np.exp(chunk * log_gamma)[np.arange(GROUP_W) // HEAD_W][None, :]
    bd = (lane_h[:, None] == (np.arange(GROUP_W) // HEAD_W)[None, :]).astype(np.float32)
    f = lambda a: jnp.asarray(a, dtype=F32)
    return f(cos), f(sin), f(decay), f(zeta), f(xi), f(gamma_c), f(bd)


def _ret_perm():
    p = np.arange(128)
    h = _ret_lane_head(p)
    return h * D_DK + (p % 16) + np.where(p >= 64, 16, 0)


def _rms_rows(x, gain):
    return x * lax.rsqrt(jnp.mean(x * x, axis=-1, keepdims=True) + EPS) * gain


def _ffn_body(x_ref, g_ref, wg_ref, wu_ref, wo_ref, o_ref, h_scr, acc_scr):
    j = pl.program_id(1)

    @pl.when(j == 0)
    def _():
        h_scr[...] = _bf(_rms_rows(x_ref[...], g_ref[...]))
        acc_scr[...] = jnp.zeros_like(acc_scr)

    h = h_scr[...]
    a = _dot(h, wg_ref[...])
    b = _dot(h, wu_ref[...])
    acc_scr[...] += _dot(_bf(_silu(a) * b), wo_ref[...])

    @pl.when(j == pl.num_programs(1) - 1)
    def _():
        o_ref[...] = x_ref[...] + 0.5 * acc_scr[...]


def _ffn(x, gain, w_in, w_out, *, tm, tf):
    n, d = x.shape
    d_ff = w_out.shape[0]
    nf = d_ff // tf
    return pl.pallas_call(
        _ffn_body,
        grid=(n // tm, nf),
        in_specs=[
            pl.BlockSpec((tm, d), lambda i, j: (i, 0)),
            pl.BlockSpec((1, d), lambda i, j: (0, 0)),
            pl.BlockSpec((d, tf), lambda i, j: (0, j)),
            pl.BlockSpec((d, tf), lambda i, j: (0, nf + j)),
            pl.BlockSpec((tf, d), lambda i, j: (j, 0)),
        ],
        out_specs=pl.BlockSpec((tm, d), lambda i, j: (i, 0)),
        out_shape=jax.ShapeDtypeStruct((n, d), F32),
        scratch_shapes=[pltpu.VMEM((tm, d), BF16), pltpu.VMEM((tm, d), F32)],
        compiler_params=pltpu.CompilerParams(
            dimension_semantics=("parallel", "arbitrary"), vmem_limit_bytes=VMEM_LIMIT),
        name="ffn",
    )(x, gain, w_in, w_in, w_out)


def _proj_body(x_ref, g_ref, wa_ref, wb_ref, wc_ref, wd_ref, oa_ref, ob_ref, oc_ref, od_ref):
    h = _bf(_rms_rows(x_ref[...], g_ref[...]))
    oa_ref[...] = _dot(h, wa_ref[...])
    ob_ref[...] = _dot(h, wb_ref[...])
    oc_ref[...] = _bf(_dot(h, wc_ref[...]))
    od_ref[...] = _dot(h, wd_ref[...])


def _mix_proj(x, gain, wa, wb, wc, wd, *, tm):
    n, d = x.shape
    ws = (wa, wb, wc, wd)
    dts = (F32, F32, BF16, F32)
    return pl.pallas_call(
        _proj_body,
        grid=(n // tm,),
        in_specs=[pl.BlockSpec((tm, d), lambda i: (i, 0)), pl.BlockSpec((1, d), lambda i: (0, 0))]
        + [pl.BlockSpec(w.shape, lambda i: (0, 0)) for w in ws],
        out_specs=[pl.BlockSpec((tm, w.shape[1]), lambda i: (i, 0)) for w in ws],
        out_shape=[jax.ShapeDtypeStruct((n, w.shape[1]), dt) for w, dt in zip(ws, dts)],
        compiler_params=pltpu.CompilerParams(
            dimension_semantics=("parallel",), vmem_limit_bytes=VMEM_LIMIT),
        name="mix_proj",
    )(x, gain, *ws)


def _mix_out_body(x_ref, ya_ref, yb_ref, yc_ref, yd_ref, w_ref, o_ref):
    acc = x_ref[...]
    for m, y_ref in enumerate((ya_ref, yb_ref, yc_ref, yd_ref)):
        acc = acc + _dot(y_ref[...], w_ref[m * GROUP_W:(m + 1) * GROUP_W, :])
    o_ref[...] = acc


def _mix_out(x, ys, w, *, tm):
    n, d = x.shape
    return pl.pallas_call(
        _mix_out_body,
        grid=(n // tm,),
        in_specs=[pl.BlockSpec((tm, d), lambda i: (i, 0))]
        + [pl.BlockSpec((tm, GROUP_W), lambda i: (i, 0)) for _ in ys]
        + [pl.BlockSpec(w.shape, lambda i: (0, 0))],
        out_specs=pl.BlockSpec((tm, d), lambda i: (i, 0)),
        out_shape=jax.ShapeDtypeStruct((n, d), F32),
        compiler_params=pltpu.CompilerParams(
            dimension_semantics=("parallel",), vmem_limit_bytes=VMEM_LIMIT),
        name="mix_out",
    )(x, *ys, w)


def _final_norm_body(x_ref, g_ref, o_ref):
    o_ref[...] = _rms_rows(x_ref[...], g_ref[...])


def _final_norm(x, gain, *, tm):
    n, d = x.shape
    return pl.pallas_call(
        _final_norm_body,
        grid=(n // tm,),
        in_specs=[pl.BlockSpec((tm, d), lambda i: (i, 0)), pl.BlockSpec((1, d), lambda i: (0, 0))],
        out_specs=pl.BlockSpec((tm, d), lambda i: (i, 0)),
        out_shape=jax.ShapeDtypeStruct((n, d), F32),
        compiler_params=pltpu.CompilerParams(dimension_semantics=("parallel",)),
        name="final_norm",
    )(x, gain)


def _hgrn_body(layer, q_ref, f_ref, i_ref, g_ref, lbl_ref, ng_ref, ctab_ref, gmat_ref, bd_ref,
               o_ref, st_scr):
    @pl.when(pl.program_id(1) == 0)
    def _():
        st_scr[...] = jnp.zeros_like(st_scr)

    logits = lbl_ref[...]
    e = jnp.exp(logits - jnp.max(logits, axis=0, keepdims=True))
    p = e / jnp.sum(e, axis=0, keepdims=True)
    lb = jnp.zeros_like(p[0:1])
    for r in range(1, layer + 1):
        lb = lb + p[r:r + 1]

    pair, upper, lower, eye = _level_masks(GROUP_W, stack=N_HEADS)
    hm = _head_masks(GROUP_W, HEAD_W, BF16)
    ctab = ctab_ref[...]
    gmat = gmat_ref[...]
    bd = bd_ref[...]
    ng = ng_ref[...]

    chunks = range(q_ref.shape[1] // CHUNK)
    rows_of = lambda c: pl.ds(c * CHUNK, CHUNK)
    qs, kk, cs = [], [], []
    for c in chunks:
        f = lb + (1.0 - lb) * _sigmoid(f_ref[0, rows_of(c), :])
        qs.append(_silu(q_ref[0, rows_of(c), :]))
        kk.append(1.0 - f)
        cs.append(_select_rows(ctab, jnp.log(f)))

    scores = []
    for c in chunks:
        sc = eye * _dot_nt(_stack_heads(_bf(qs[c]), hm), _bf(kk[c]))
        for l, (fq, fk) in enumerate(_decay_factors(cs[c], upper, lower)):
            sc = sc + pair[l] * _dot_nt(_stack_heads(_bf(qs[c] * fq), hm), _bf(kk[c] * fk))
        scores.append(_bf(sc))

    qdec, b_last, o_intra, st_add = [], [], [], []
    for c in chunks:
        b = cs[c][0:CHUNK]
        rem = cs[c][CHUNK:2 * CHUNK]
        v = _bf(_silu(i_ref[0, rows_of(c), :]))
        o = _dot(scores[c][0:CHUNK], v * hm[0])
        for h in range(1, N_HEADS):
            o = o + _dot(scores[c][h * CHUNK:(h + 1) * CHUNK], v * hm[h])
        o_intra.append(o)
        qdec.append(_bf(qs[c] * jnp.exp(b)))
        st_add.append(_dot_tn(v, _bf(kk[c] * jnp.exp(rem))) * bd)
        b_last.append(jnp.exp(b[CHUNK - 1:CHUNK]))

    st = st_scr[...]
    for c in chunks:
        sl = pl.ds(c * CHUNK, CHUNK)
        o = o_intra[c] + _dot_nt(qdec[c], _bf(st))
        st = st * b_last[c] + st_add[c]
        y = _head_rmsnorm(o, gmat, ng) * _sigmoid(g_ref[0, sl, :])
        o_ref[0, sl, :] = _bf(y)
    st_scr[...] = st


def _hgrn_mixer(proj_a, lb_logits, norm_g, ctab, gmat, bd, layer, rows):
    bsz, seq, _ = proj_a.shape
    col = lambda c: pl.BlockSpec((1, rows, GROUP_W), lambda b, t, c=c: (b, t, c))
    whole = lambda a: pl.BlockSpec(a.shape, lambda b, t: (0,) * a.ndim)
    return pl.pallas_call(
        functools.partial(_hgrn_body, layer),
        grid=(bsz, seq // rows),
        in_specs=[col(0), col(1), col(2), col(3), whole(lb_logits), whole(norm_g),
                  whole(ctab), whole(gmat), whole(bd)],
        out_specs=pl.BlockSpec((1, rows, GROUP_W), lambda b, t: (b, t, 0)),
        out_shape=jax.ShapeDtypeStruct((bsz, seq, GROUP_W), BF16),
        scratch_shapes=[pltpu.VMEM((GROUP_W, GROUP_W), F32)],
        compiler_params=pltpu.CompilerParams(
            dimension_semantics=("parallel", "arbitrary"), vmem_limit_bytes=VMEM_LIMIT),
        name="hgrn2",
    )(proj_a, proj_a, proj_a, proj_a, lb_logits, norm_g, ctab, gmat, bd)


def _gdn_body(q_ref, k_ref, v_ref, g_ref, a_ref, bl_ref, cw_ref, alog_ref, dtb_ref, ng_ref,
              ctab_ref, gmat_ref, bd_ref, o_ref, s_scr, tail_scr, ext_scr):
    @pl.when(pl.program_id(1) == 0)
    def _():
        s_scr[...] = jnp.zeros_like(s_scr)
        tail_scr[...] = jnp.zeros_like(tail_scr)

    pair, upper, lower, eye = _level_masks(GROUP_W)
    hm = _head_masks(GROUP_W, HEAD_W, BF16)
    hmf = _head_masks(GROUP_W, HEAD_W, F32)
    ctab = ctab_ref[...]
    gmat = gmat_ref[...]
    bd = bd_ref[...]
    ng = ng_ref[...]
    neg_a = -jnp.exp(alog_ref[...])
    dtb = dtb_ref[...]
    raw = (q_ref, k_ref, v_ref)
    rows = q_ref.shape[1]

    for j in range(3):
        ext_scr[0:8, j * GROUP_W:(j + 1) * GROUP_W] = tail_scr[:, j * GROUP_W:(j + 1) * GROUP_W]
        ext_scr[8:8 + CHUNK, j * GROUP_W:(j + 1) * GROUP_W] = raw[j][0, 0:CHUNK, :]
        tail_scr[:, j * GROUP_W:(j + 1) * GROUP_W] = raw[j][0, rows - 8:rows, :]

    def conv(c, j):
        w = cw_ref[:, j * GROUP_W:(j + 1) * GROUP_W]
        acc = None
        for d in range(4):
            if c == 0:
                x = ext_scr[pl.ds(8 - d, CHUNK), j * GROUP_W:(j + 1) * GROUP_W]
            else:
                x = raw[j][0, pl.ds(c * CHUNK - d, CHUNK), :]
            term = x * w[3 - d:4 - d, :]
            acc = term if acc is None else acc + term
        return _silu(acc)

    nch = rows // CHUNK
    pairs = [(c, h) for c in range(nch) for h in range(N_HEADS)]
    rr = lax.broadcasted_iota(jnp.int32, (N_HEADS * CHUNK, CHUNK), 0) & (CHUNK - 1)
    cc = lax.broadcasted_iota(jnp.int32, (N_HEADS * CHUNK, CHUNK), 1)
    tril = rr >= cc
    tril_f = tril.astype(F32)
    stril_f = (rr > cc).astype(F32)
    sub = lax.broadcasted_iota(jnp.int32, (1, GROUP_W), 1) & (HEAD_W - 1)
    lane_is = [(sub == i).astype(BF16) for i in range(6)]

    qc, kc, vc, qn, kn, cs = [], [], [], [], [], []
    for c in range(nch):
        qc.append(conv(c, 0))
        kc.append(conv(c, 1))
        vc.append(conv(c, 2))
        qn.append(_headsum(qc[c] * qc[c], gmat))
        kn.append(_headsum(kc[c] * kc[c], gmat))
        log_alpha = neg_a * _softplus(a_ref[0, pl.ds(c * CHUNK, CHUNK), :] + dtb)
        cs.append(_select_rows(ctab[0:2 * CHUNK], log_alpha))

    diff, prod, kb, beta = [], [], [], []
    for c in range(nch):
        qc[c] = qc[c] * lax.rsqrt(qn[c] + EPS) * (HEAD_W ** -0.5)
        kc[c] = kc[c] * lax.rsqrt(kn[c] + EPS)
        beta.append(_sigmoid(bl_ref[0, pl.ds(c * CHUNK, CHUNK), :]))
        kb.append(kc[c] * beta[c])
        g1, g2, g3 = _split3(cs[c][0:CHUNK])
        lhs = g1 * lane_is[0] + g2 * lane_is[1] + g3 * lane_is[2] + (lane_is[3] + lane_is[4] + lane_is[5])
        rhs = (lane_is[0] + lane_is[1] + lane_is[2]) - g1 * lane_is[3] - g2 * lane_is[4] - g3 * lane_is[5]
        diff.append(_dot_nt(_stack_heads(lhs, hm), rhs))
        prod.append(_dot_nt(jnp.concatenate([_stack_heads(_bf(qc[c]), hm), _stack_heads(_bf(kb[c]), hm)],
                                            axis=0), _bf(kc[c])))

    qg, kdec, vb, kbg, g_last, attn, m_mat = [], [], [], [], [], {}, {}
    for c in range(nch):
        gc = cs[c][0:CHUNK]
        rem = cs[c][CHUNK:2 * CHUNK]
        dec = jnp.exp(jnp.where(tril, diff[c], 0.0))
        attn_st = _bf(tril_f * dec * prod[c][0:N_HEADS * CHUNK])
        m_st = stril_f * dec * prod[c][N_HEADS * CHUNK:2 * N_HEADS * CHUNK]
        for h in range(N_HEADS):
            attn[c, h] = attn_st[h * CHUNK:(h + 1) * CHUNK]
            m_mat[c, h] = m_st[h * CHUNK:(h + 1) * CHUNK]
        qg.append(qc[c] * jnp.exp(gc))
        kdec.append(_bf(kc[c] * jnp.exp(rem)))
        vb.append(vc[c] * beta[c])
        kbg.append(kb[c] * jnp.exp(gc))
        g_last.append(jnp.exp(gc[CHUNK - 1:CHUNK]))

    t_inv = {p: eye - pair[N_LEVELS - 1] * m_mat[p] for p in pairs}
    for l in range(N_LEVELS - 2, -1, -1):
        t_b = {p: _bf(t_inv[p]) for p in pairs}
        tl = {p: _dot(t_b[p], _bf(pair[l] * m_mat[p])) for p in pairs}
        t_inv = {p: t_inv[p] - _dot(_bf(tl[p]), t_b[p]) for p in pairs}

    uw = []
    for c in range(nch):
        rhs = jnp.concatenate([vb[c], kbg[c]], axis=1)
        acc = None
        for h in range(N_HEADS):
            lhs = jnp.concatenate(_split2(t_inv[c, h]), axis=1)
            rhs_h = _bf(rhs * jnp.concatenate([hmf[h], hmf[h]], axis=1))
            term = _dot(lhs, jnp.concatenate([rhs_h, rhs_h], axis=0))
            acc = term if acc is None else acc + term
        uw.append(_bf(acc))

    o_lhs, o_add, s_lhs, s_add = [], [], [], []
    for c in range(nch):
        u_b = uw[c][:, 0:GROUP_W]
        w_b = uw[c][:, GROUP_W:2 * GROUP_W]
        aw = None
        for h in range(N_HEADS):
            term = _dot(attn[c, h], jnp.concatenate([u_b * hm[h], w_b * hm[h]], axis=1))
            aw = term if aw is None else aw + term
        o_add.append(aw[:, 0:GROUP_W])
        o_lhs.append(_bf(qg[c] - aw[:, GROUP_W:2 * GROUP_W]))
        ks = _dot_tn(kdec[c], uw[c])
        s_add.append(ks[:, 0:GROUP_W] * bd)
        s_lhs.append(_bf(-ks[:, GROUP_W:2 * GROUP_W] * bd))

    s = s_scr[...]
    for c in range(nch):
        sl = pl.ds(c * CHUNK, CHUNK)
        s_b = _bf(s)
        o = o_add[c] + _dot(o_lhs[c], s_b)
        s = s * g_last[c] + _dot(s_lhs[c], s_b) + s_add[c]
        y = _head_rmsnorm(o, gmat, ng) * _silu(g_ref[0, sl, :])
        o_ref[0, sl, :] = _bf(y)
    s_scr[...] = s


def _gdn_mixer(proj_b, conv_w, a_log_rep, dt_bias_rep, norm_g, ctab, gmat, bd, rows):
    bsz, seq, _ = proj_b.shape
    col = lambda c: pl.BlockSpec((1, rows, GROUP_W), lambda b, t, c=c: (b, t, c))
    whole = lambda a: pl.BlockSpec(a.shape, lambda b, t: (0,) * a.ndim)
    return pl.pallas_call(
        _gdn_body,
        grid=(bsz, seq // rows),
        in_specs=[col(0), col(1), col(2), col(3), col(4), col(5), whole(conv_w), whole(a_log_rep),
                  whole(dt_bias_rep), whole(norm_g), whole(ctab), whole(gmat), whole(bd)],
        out_specs=pl.BlockSpec((1, rows, GROUP_W), lambda b, t: (b, t, 0)),
        out_shape=jax.ShapeDtypeStruct((bsz, seq, GROUP_W), BF16),
        scratch_shapes=[pltpu.VMEM((GROUP_W, GROUP_W), F32),
                        pltpu.VMEM((8, 3 * GROUP_W), F32),
                        pltpu.VMEM((8 + CHUNK, 3 * GROUP_W), F32)],
        compiler_params=pltpu.CompilerParams(
            dimension_semantics=("parallel", "arbitrary"), vmem_limit_bytes=VMEM_LIMIT),
        name="gdn",
    )(proj_b, proj_b, proj_b, proj_b, proj_b, proj_b, conv_w, a_log_rep, dt_bias_rep, norm_g,
      ctab, gmat, bd)


def _sb_body(q_ref, k_ref, v_ref, ng_ref, later_ref, gmat_ref, o_ref):
    qi = pl.program_id(1)
    blk, unit = SB_BLOCK, SB_UNIT
    hm = _head_masks(GROUP_W, HEAD_W, BF16)
    later = later_ref[...]
    q = q_ref[0] * (HEAD_W ** -0.5)
    qs = jnp.concatenate([q * hm[h] for h in range(N_HEADS)], axis=0)
    t_loc = lax.broadcasted_iota(jnp.int32, (N_HEADS * blk, unit), 0) & (blk - 1)
    s_loc = lax.broadcasted_iota(jnp.int32, (N_HEADS * blk, unit), 1)

    def sweep(u, carry, acc, diagonal):
        kblk = k_ref[0, pl.ds(u * unit, unit), :]
        vblk = v_ref[0, pl.ds(u * unit, unit), :]
        z = _dot_nt(qs, kblk)
        l1p = jnp.log(1.0 + jnp.exp(-jnp.abs(z)))
        log_sig = jnp.minimum(z, 0.0) - l1p
        log_stay = -jnp.maximum(z, 0.0) - l1p
        if diagonal:
            past = (qi * blk + t_loc) > (u * unit + s_loc)
            log_stay = jnp.where(past, log_stay, 0.0)
        sfx = _dot(_bf(log_stay), later)
        wts = jnp.exp(log_sig + sfx + carry)
        if diagonal:
            wts = jnp.where(past, wts, 0.0)
        wts = _bf(wts)
        w_cat = jnp.concatenate([wts[h * blk:(h + 1) * blk] for h in range(N_HEADS)], axis=1)
        v_cat = jnp.concatenate([vblk * hm[h] for h in range(N_HEADS)], axis=0)
        acc = acc + _dot(w_cat, v_cat)
        return carry + jnp.sum(log_stay, axis=-1, keepdims=True), acc

    u0 = (qi * blk) // unit
    carry, acc = sweep(u0, jnp.zeros((N_HEADS * blk, 1), F32), jnp.zeros((blk, GROUP_W), F32), True)

    def step(i, state):
        return sweep(u0 - 1 - i, state[0], state[1], False)

    carry, acc = lax.fori_loop(0, u0, step, (carry, acc))
    o_ref[0] = _bf(_head_rmsnorm(acc, gmat_ref[...], ng_ref[...]))


def _sb_mixer(proj_c, norm_g, u2, gmat):
    bsz, seq, _ = proj_c.shape
    whole = lambda a: pl.BlockSpec(a.shape, lambda b, t: (0,) * a.ndim)
    return pl.pallas_call(
        _sb_body,
        grid=(bsz, seq // SB_BLOCK),
        in_specs=[pl.BlockSpec((1, SB_BLOCK, GROUP_W), lambda b, t: (b, t, 0)),
                  pl.BlockSpec((1, seq, GROUP_W), lambda b, t: (b, 0, 1)),
                  pl.BlockSpec((1, seq, GROUP_W), lambda b, t: (b, 0, 2)),
                  whole(norm_g), whole(u2), whole(gmat)],
        out_specs=pl.BlockSpec((1, SB_BLOCK, GROUP_W), lambda b, t: (b, t, 0)),
        out_shape=jax.ShapeDtypeStruct((bsz, seq, GROUP_W), BF16),
        compiler_params=pltpu.CompilerParams(
            dimension_semantics=("parallel", "arbitrary"), vmem_limit_bytes=VMEM_LIMIT),
        name="stickbreak",
    )(proj_c, proj_c, proj_c, norm_g, u2, gmat)


def _sb_table():
    j = np.arange(SB_UNIT)[:, None]
    s = np.arange(SB_UNIT)[None, :]
    return jnp.asarray((j > s).astype(np.float32), dtype=BF16)


def _ret_body(q_ref, k_ref, v_ref, g_ref, cos_ref, sin_ref, decay_ref, zeta_ref, xi_ref, gc_ref,
              bd_ref, ng_ref, gmat_ref, o_ref, s_scr):
    @pl.when(pl.program_id(1) == 0)
    def _():
        s_scr[...] = jnp.zeros_like(s_scr)

    chunk = decay_ref.shape[1]
    decay = decay_ref[...]
    lane = lax.broadcasted_iota(jnp.int32, (1, 2 * HEAD_W), 1)
    hq = [(((lane % 64) // 16) == h).astype(BF16) for h in range(N_HEADS)]
    hv = _head_masks(GROUP_W, HEAD_W, BF16)
    gmat = gmat_ref[...]
    ng = ng_ref[...]
    bd = bd_ref[...]
    zeta = zeta_ref[...]
    xi = xi_ref[...]
    gamma_c = gc_ref[...]

    nch = q_ref.shape[1] // chunk
    qx, o_intra, s_add = [], [], []
    for c in range(nch):
        sl = pl.ds(c * chunk, chunk)
        cos = cos_ref[sl, :]
        sin = sin_ref[sl, :]
        q = q_ref[0, sl, :]
        k = k_ref[0, sl, :]
        qr = q * cos + pltpu.roll(q, 64, axis=1) * sin
        kr = (k * cos + pltpu.roll(k, 64, axis=1) * sin) * (D_DK ** -0.5)
        v = _bf(v_ref[0, sl, :])
        sc = _bf(_dot_nt(_stack_heads(_bf(qr), hq), _bf(kr)) * decay)
        o = _dot(sc[0:chunk], v * hv[0])
        for h in range(1, N_HEADS):
            o = o + _dot(sc[h * chunk:(h + 1) * chunk], v * hv[h])
        o_intra.append(o)
        qx.append(_bf(qr * xi))
        s_add.append(_dot_tn(_bf(kr * zeta), v) * bd)

    s = s_scr[...]
    for c in range(nch):
        sl = pl.ds(c * chunk, chunk)
        o = o_intra[c] + _dot(qx[c], _bf(s))
        s = s * gamma_c + s_add[c]
        y = _head_rmsnorm(o, gmat, ng) * _silu(g_ref[0, sl, :])
        o_ref[0, sl, :] = _bf(y)
    s_scr[...] = s


def _ret_mixer(proj_d, tables, norm_g, gmat, rows):
    bsz, seq, _ = proj_d.shape
    cos, sin, decay, zeta, xi, gamma_c, bd = tables
    whole = lambda a: pl.BlockSpec(a.shape, lambda b, t: (0,) * a.ndim)
    return pl.pallas_call(
        _ret_body,
        grid=(bsz, seq // rows),
        in_specs=[pl.BlockSpec((1, rows, 128), lambda b, t: (b, t, 0)),
                  pl.BlockSpec((1, rows, 128), lambda b, t: (b, t, 1)),
                  pl.BlockSpec((1, rows, GROUP_W), lambda b, t: (b, t, 1)),
                  pl.BlockSpec((1, rows, GROUP_W), lambda b, t: (b, t, 2)),
                  pl.BlockSpec((rows, 128), lambda b, t: (t, 0)),
                  pl.BlockSpec((rows, 128), lambda b, t: (t, 0)),
                  whole(decay), whole(zeta), whole(xi), whole(gamma_c), whole(bd),
                  whole(norm_g), whole(gmat)],
        out_specs=pl.BlockSpec((1, rows, GROUP_W), lambda b, t: (b, t, 0)),
        out_shape=jax.ShapeDtypeStruct((bsz, seq, GROUP_W), BF16),
        scratch_shapes=[pltpu.VMEM((2 * HEAD_W, GROUP_W), F32)],
        compiler_params=pltpu.CompilerParams(
            dimension_semantics=("parallel", "arbitrary"), vmem_limit_bytes=VMEM_LIMIT),
        name="retention",
    )(proj_d, proj_d, proj_d, proj_d, cos, sin, decay, zeta, xi, gamma_c, bd, norm_g, gmat)


def _mix_weights(w_in):
    a_w = w_in[:, 0:1024]
    b0 = 1024
    b_main = w_in[:, b0:b0 + 1024]
    b_a = jnp.repeat(w_in[:, b0 + 1024:b0 + 1028], HEAD_W, axis=1)
    b_b = jnp.repeat(w_in[:, b0 + 1028:b0 + 1032], HEAD_W, axis=1)
    c0 = b0 + 1032
    c_w = w_in[:, c0:c0 + 768]
    d0 = c0 + 768
    perm = _ret_perm()
    d_q = w_in[:, d0:d0 + 128][:, perm]
    d_k = w_in[:, d0 + 128:d0 + 256][:, perm]
    d_rest = w_in[:, d0 + 256:d0 + 768]
    b_w = jnp.concatenate([b_main, b_a, b_b], axis=1)
    d_w = jnp.concatenate([d_q, d_k, d_rest], axis=1)
    return _bf(a_w), _bf(b_w), _bf(c_w), _bf(d_w)


def kernel(x, ffn1_norm, ffn1_w_in, ffn1_w_out, mix_norm, mix_w_in, mix_w_out, ffn2_norm, ffn2_w_in,
           ffn2_w_out, hgrn_lb_logits, hgrn_out_norm, gdn_conv_w, gdn_a_log, gdn_dt_bias, gdn_out_norm,
           sb_out_norm, ret_out_norm, final_norm):
    bsz, seq, d = x.shape
    depth = ffn1_norm.shape[0]
    n = bsz * seq
    d_ff = ffn1_w_out.shape[1]
    tm_ffn, tf = 512, d_ff // 2
    tm = 256
    rows = MIXER_ROWS

    ctab = _cumsum_table()
    gmat = _head_block_matrix(GROUP_W, HEAD_W, GROUP_W, HEAD_W, BF16)
    bd = _head_block_matrix(GROUP_W, HEAD_W, GROUP_W, HEAD_W, F32)
    u2 = _sb_table()
    ret_tables = _retention_tables(seq, CHUNK)
    row = lambda a: a.reshape(1, -1).astype(F32)
    rep = lambda a: jnp.repeat(a.astype(F32), HEAD_W).reshape(1, GROUP_W)

    x = x.reshape(n, d)
    for l in range(depth):
        x = _ffn(x, row(ffn1_norm[l]), _bf(ffn1_w_in[l]), _bf(ffn1_w_out[l]), tm=tm_ffn, tf=tf)
        pa, pb, pc, pd = _mix_proj(x, row(mix_norm[l]), *_mix_weights(mix_w_in[l]), tm=tm)
        shp = lambda a: a.reshape(bsz, seq, a.shape[-1])
        ya = _hgrn_mixer(shp(pa), hgrn_lb_logits.astype(F32), row(hgrn_out_norm[l]), ctab, gmat, bd, l, rows)
        yb = _gdn_mixer(shp(pb), gdn_conv_w[l].astype(F32), rep(gdn_a_log[l]), rep(gdn_dt_bias[l]),
                        row(gdn_out_norm[l]), ctab, gmat, bd, rows)
        yc = _sb_mixer(shp(pc), row(sb_out_norm[l]), u2, gmat)
        yd = _ret_mixer(shp(pd), ret_tables, row(ret_out_norm[l]), gmat, rows)
        ys = [y.reshape(n, GROUP_W) for y in (ya, yb, yc, yd)]
        x = _mix_out(x, ys, _bf(mix_w_out[l]), tm=tm)
        x = _ffn(x, row(ffn2_norm[l]), _bf(ffn2_w_in[l]), _bf(ffn2_w_out[l]), tm=tm_ffn, tf=tf)
    return _final_norm(x, row(final_norm), tm=tm).reshape(bsz, seq, d)
```

```python
import functools
import math

import numpy as np
import jax
import jax.numpy as jnp
from jax import lax
from jax.experimental import pallas as pl
from jax.experimental.pallas import tpu as pltpu

F32 = jnp.float32
BF16 = jnp.bfloat16

EPS = 1e-6
CHUNK = 64
N_HEADS = 4
HEAD_W = 64
GROUP_W = N_HEADS * HEAD_W
D_DK = 32
ROPE_BASE = 10000.0
N_LEVELS = 6
VMEM_LIMIT = 56 * 1024 * 1024

MIXER_ROWS = 256
FFN_COLS = 256
SB_BLOCK = 128
SB_UNIT = 256
SB_DEAD = -104.0


def _bf(x):
    return x.astype(BF16)


def _dot(a, b):
    return jnp.dot(a, b, preferred_element_type=F32)


def _dot_nt(a, b):
    return lax.dot_general(a, b, (((1,), (1,)), ((), ())), preferred_element_type=F32)


def _dot_tn(a, b):
    return lax.dot_general(a, b, (((0,), (0,)), ((), ())), preferred_element_type=F32)


def _split2(x):
    hi = _bf(x)
    lo = _bf(x - hi.astype(F32))
    return hi, lo


def _split3(x):
    h1 = _bf(x)
    r = x - h1.astype(F32)
    h2 = _bf(r)
    h3 = _bf(r - h2.astype(F32))
    return h1, h2, h3


def _select_rows(w3, x):
    return _dot(w3, jnp.concatenate(_split3(x), axis=0))


def _headsum(x, gmat):
    hi, lo = _split2(x)
    return _dot(hi, gmat) + _dot(lo, gmat)


def _head_rmsnorm(o, gmat, gain):
    ms = _headsum(o * o, gmat) * (1.0 / HEAD_W)
    return o * lax.rsqrt(ms + EPS) * gain


def _sigmoid(x):
    return 1.0 / (1.0 + jnp.exp(-x))


def _silu(x):
    return x * _sigmoid(x)


def _softplus(x):
    return jnp.maximum(x, 0.0) + jnp.log(1.0 + jnp.exp(-jnp.abs(x)))


def _head_masks(width, lanes_per_head, dtype):
    lane = lax.broadcasted_iota(jnp.int32, (1, width), 1)
    return [((lane // lanes_per_head) == h).astype(dtype) for h in range(N_HEADS)]


def _stack_heads(x, masks):
    return jnp.concatenate([x * m for m in masks], axis=0)


def _level_masks(width, stack=1):
    r = lax.broadcasted_iota(jnp.int32, (stack * CHUNK, CHUNK), 0) & (CHUNK - 1)
    c = lax.broadcasted_iota(jnp.int32, (stack * CHUNK, CHUNK), 1)
    rows = lax.broadcasted_iota(jnp.int32, (CHUNK, width), 0)
    pair, upper, lower = [], [], []
    for l in range(N_LEVELS):
        n = CHUNK >> l
        m = n // 2
        sh = N_LEVELS - l
        same = (r >> sh) == (c >> sh)
        pair.append((same & ((r & (n - 1)) >= m) & ((c & (n - 1)) < m)).astype(F32))
        upper.append((rows & (n - 1)) >= m)
        lower.append((rows & (n - 1)) < m)
    eye = (r == c).astype(F32)
    return pair, upper, lower, eye


def _decay_factors(cs, upper, lower):
    b = cs[0:CHUNK]
    out = []
    for l in range(N_LEVELS):
        ab = cs[(2 + l) * CHUNK:(3 + l) * CHUNK]
        fq = jnp.exp(jnp.where(upper[l], b - ab, 0.0))
        fk = jnp.exp(jnp.where(lower[l], ab - b, 0.0))
        out.append((fq, fk))
    return out


def _cumsum_table():
    t = np.arange(CHUNK)
    tri = (t[None, :] <= t[:, None]).astype(np.float32)
    up = (t[None, :] > t[:, None]).astype(np.float32)
    blocks = [tri, up]
    for l in range(N_LEVELS):
        n = CHUNK >> l
        anchor = (t // n) * n + n // 2 - 1
        blocks.append(tri[anchor])
    tab = np.concatenate(blocks, axis=0)
    return jnp.asarray(np.concatenate([tab, tab, tab], axis=1), dtype=BF16)


def _head_block_matrix(rows, rows_per_head, cols, cols_per_head, dtype):
    r = np.arange(rows)[:, None] // rows_per_head
    c = np.arange(cols)[None, :] // cols_per_head
    return jnp.asarray((r == c).astype(np.float32), dtype=dtype)


def _ret_lane_head(p):
    return (p % 64) // 16


def _retention_tables(seq, chunk):
    p = np.arange(128)
    half = D_DK // 2
    inv_freq = ROPE_BASE ** (-(p % 16).astype(np.float64) / half)
    ang = np.arange(seq, dtype=np.float64)[:, None] * inv_freq[None, :]
    cos = np.cos(ang)
    sin = np.sin(ang) * np.where(p < 64, -1.0, 1.0)[None, :]
    log_gamma = np.log(1.0 - 2.0 ** (-5.0 - np.arange(N_HEADS, dtype=np.float64)))
    c = np.arange(chunk, dtype=np.float64)
    rel = c[:, None] - c[None, :]
    decay = np.where(rel[None] >= 0, np.exp(rel[None] * log_gamma[:, None, None]), 0.0)
    decay = decay.reshape(N_HEADS * chunk, chunk)
    lane_h = _ret_lane_head(p)
    zeta = np.exp((chunk - 1 - c)[:, None] * log_gamma[lane_h][None, :])
    xi = np.exp((c + 1.0)[:, None] * log_gamma[lane_h][None, :])
    gamma_c = np.exp(chunk * log_gamma)[np.arange(GROUP_W) // HEAD_W][None, :]
    bd = (lane_h[:, None] == (np.arange(GROUP_W) // HEAD_W)[None, :]).astype(np.float32)
    f = lambda a: jnp.asarray(a, dtype=F32)
    return f(cos), f(sin), f(decay), f(zeta), f(xi), f(gamma_c), f(bd)


def _ret_perm():
    p = np.arange(128)
    h = _ret_lane_head(p)
    return h * D_DK + (p % 16) + np.where(p >= 64, 16, 0)


def _rms_rows(x, gain):
    return x * lax.rsqrt(jnp.mean(x * x, axis=-1, keepdims=True) + EPS) * gain


def _ffn_body(has_mix, has_final, *refs):
    refs = list(refs)
    x_ref = refs.pop(0)
    y_refs = [refs.pop(0) for _ in range(4)] if has_mix else []
    wmix_ref = refs.pop(0) if has_mix else None
    g_ref, win_ref, wout_ref = refs.pop(0), refs.pop(0), refs.pop(0)
    gf_ref = refs.pop(0) if has_final else None
    o_ref = refs.pop(0)

    x = x_ref[...]
    for m, y_ref in enumerate(y_refs):
        x = x + _dot(y_ref[...], wmix_ref[m * GROUP_W:(m + 1) * GROUP_W, :])
    h = _bf(_rms_rows(x, g_ref[...]))

    d_ff = wout_ref.shape[0]
    steps = d_ff // FFN_COLS
    acc = None
    act = None
    for c in range(steps + 1):
        if c < steps:
            gate = _dot(h, win_ref[:, c * FFN_COLS:(c + 1) * FFN_COLS])
            up = _dot(h, win_ref[:, d_ff + c * FFN_COLS:d_ff + (c + 1) * FFN_COLS])
        if act is not None:
            down = _dot(act, wout_ref[(c - 1) * FFN_COLS:c * FFN_COLS, :])
            acc = down if acc is None else acc + down
        if c < steps:
            act = _bf(_silu(gate) * up)
    y = x + 0.5 * acc
    if has_final:
        y = _rms_rows(y, gf_ref[...])
    o_ref[...] = y


def _ffn(x, gain, w_in, w_out, *, tm, mix=None, final_gain=None):
    n, d = x.shape
    rows = lambda w: pl.BlockSpec((tm, w), lambda i: (i, 0))
    whole = lambda a: pl.BlockSpec(a.shape, lambda i: (0, 0), pipeline_mode=pl.Buffered(1))
    args, specs = [x], [rows(d)]
    if mix is not None:
        ys, w_mix = mix
        args += list(ys) + [w_mix]
        specs += [rows(GROUP_W) for _ in ys] + [whole(w_mix)]
    args += [gain, w_in, w_out]
    specs += [whole(gain), whole(w_in), whole(w_out)]
    if final_gain is not None:
        args.append(final_gain)
        specs.append(whole(final_gain))
    return pl.pallas_call(
        functools.partial(_ffn_body, mix is not None, final_gain is not None),
        grid=(n // tm,),
        in_specs=specs,
        out_specs=rows(d),
        out_shape=jax.ShapeDtypeStruct((n, d), F32),
        compiler_params=pltpu.CompilerParams(
            dimension_semantics=("parallel",), vmem_limit_bytes=VMEM_LIMIT),
        name="ffn",
    )(*args)


def _proj_body(x_ref, g_ref, wa_ref, wb_ref, wc_ref, wd_ref, oa_ref, ob_ref, oc_ref, od_ref):
    h = _bf(_rms_rows(x_ref[...], g_ref[...]))
    oa_ref[...] = _dot(h, wa_ref[...])
    ob_ref[...] = _dot(h, wb_ref[...])
    oc_ref[...] = _bf(_dot(h, wc_ref[...]))
    od_ref[...] = _dot(h, wd_ref[...])


def _mix_proj(x, gain, wa, wb, wc, wd, *, tm):
    n, d = x.shape
    ws = (wa, wb, wc, wd)
    dts = (F32, F32, BF16, F32)
    return pl.pallas_call(
        _proj_body,
        grid=(n // tm,),
        in_specs=[pl.BlockSpec((tm, d), lambda i: (i, 0)), pl.BlockSpec((1, d), lambda i: (0, 0))]
        + [pl.BlockSpec(w.shape, lambda i: (0, 0)) for w in ws],
        out_specs=[pl.BlockSpec((tm, w.shape[1]), lambda i: (i, 0)) for w in ws],
        out_shape=[jax.ShapeDtypeStruct((n, w.shape[1]), dt) for w, dt in zip(ws, dts)],
        compiler_params=pltpu.CompilerParams(
            dimension_semantics=("parallel",), vmem_limit_bytes=VMEM_LIMIT),
        name="mix_proj",
    )(x, gain, *ws)


def _hgrn_body(layer, q_ref, f_ref, i_ref, g_ref, lbl_ref, ng_ref, ctab_ref, gmat_ref, bd_ref,
               o_ref, st_scr):
    @pl.when(pl.program_id(1) == 0)
    def _():
        st_scr[...] = jnp.zeros_like(st_scr)

    logits = lbl_ref[...]
    e = jnp.exp(logits - jnp.max(logits, axis=0, keepdims=True))
    p = e / jnp.sum(e, axis=0, keepdims=True)
    lb = jnp.zeros_like(p[0:1])
    for r in range(1, layer + 1):
        lb = lb + p[r:r + 1]

    pair, upper, lower, eye = _level_masks(GROUP_W, stack=N_HEADS)
    hm = _head_masks(GROUP_W, HEAD_W, BF16)
    ctab = ctab_ref[...]
    gmat = gmat_ref[...]
    bd = bd_ref[...]
    ng = ng_ref[...]

    chunks = range(q_ref.shape[1] // CHUNK)
    rows_of = lambda c: pl.ds(c * CHUNK, CHUNK)
    qs, kk, cs = [], [], []
    for c in chunks:
        f = lb + (1.0 - lb) * _sigmoid(f_ref[0, rows_of(c), :])
        qs.append(_silu(q_ref[0, rows_of(c), :]))
        kk.append(1.0 - f)
        cs.append(_select_rows(ctab, jnp.log(f)))

    scores = []
    for c in chunks:
        sc = eye * _dot_nt(_stack_heads(_bf(qs[c]), hm), _bf(kk[c]))
        for l, (fq, fk) in enumerate(_decay_factors(cs[c], upper, lower)):
            sc = sc + pair[l] * _dot_nt(_stack_heads(_bf(qs[c] * fq), hm), _bf(kk[c] * fk))
        scores.append(_bf(sc))

    qdec, b_last, o_intra, st_add = [], [], [], []
    for c in chunks:
        b = cs[c][0:CHUNK]
        rem = cs[c][CHUNK:2 * CHUNK]
        v = _bf(_silu(i_ref[0, rows_of(c), :]))
        o = _dot(scores[c][0:CHUNK], v * hm[0])
        for h in range(1, N_HEADS):
            o = o + _dot(scores[c][h * CHUNK:(h + 1) * CHUNK], v * hm[h])
        o_intra.append(o)
        qdec.append(_bf(qs[c] * jnp.exp(b)))
        st_add.append(_dot_tn(v, _bf(kk[c] * jnp.exp(rem))) * bd)
        b_last.append(jnp.exp(b[CHUNK - 1:CHUNK]))

    st = st_scr[...]
    for c in chunks:
        sl = pl.ds(c * CHUNK, CHUNK)
        o = o_intra[c] + _dot_nt(qdec[c], _bf(st))
        st = st * b_last[c] + st_add[c]
        y = _head_rmsnorm(o, gmat, ng) * _sigmoid(g_ref[0, sl, :])
        o_ref[0, sl, :] = _bf(y)
    st_scr[...] = st


def _hgrn_mixer(proj_a, lb_logits, norm_g, ctab, gmat, bd, layer, rows):
    bsz, seq, _ = proj_a.shape
    col = lambda c: pl.BlockSpec((1, rows, GROUP_W), lambda b, t, c=c: (b, t, c))
    whole = lambda a: pl.BlockSpec(a.shape, lambda b, t: (0,) * a.ndim)
    return pl.pallas_call(
        functools.partial(_hgrn_body, layer),
        grid=(bsz, seq // rows),
        in_specs=[col(0), col(1), col(2), col(3), whole(lb_logits), whole(norm_g),
                  whole(ctab), whole(gmat), whole(bd)],
        out_specs=pl.BlockSpec((1, rows, GROUP_W), lambda b, t: (b, t, 0)),
        out_shape=jax.ShapeDtypeStruct((bsz, seq, GROUP_W), BF16),
        scratch_shapes=[pltpu.VMEM((GROUP_W, GROUP_W), F32)],
        compiler_params=pltpu.CompilerParams(
            dimension_semantics=("parallel", "arbitrary"), vmem_limit_bytes=VMEM_LIMIT),
        name="hgrn2",
    )(proj_a, proj_a, proj_a, proj_a, lb_logits, norm_g, ctab, gmat, bd)


def _gdn_body(q_ref, k_ref, v_ref, g_ref, a_ref, bl_ref, cw_ref, alog_ref, dtb_ref, ng_ref,
              ctab_ref, gmat_ref, bd_ref, o_ref, s_scr, tail_scr, ext_scr):
    @pl.when(pl.program_id(1) == 0)
    def _():
        s_scr[...] = jnp.zeros_like(s_scr)
        tail_scr[...] = jnp.zeros_like(tail_scr)

    pair, upper, lower, eye = _level_masks(GROUP_W)
    hm = _head_masks(GROUP_W, HEAD_W, BF16)
    hmf = _head_masks(GROUP_W, HEAD_W, F32)
    ctab = ctab_ref[...]
    gmat = gmat_ref[...]
    bd = bd_ref[...]
    ng = ng_ref[...]
    neg_a = -jnp.exp(alog_ref[...])
    dtb = dtb_ref[...]
    raw = (q_ref, k_ref, v_ref)
    rows = q_ref.shape[1]

    for j in range(3):
        ext_scr[0:8, j * GROUP_W:(j + 1) * GROUP_W] = tail_scr[:, j * GROUP_W:(j + 1) * GROUP_W]
        ext_scr[8:8 + CHUNK, j * GROUP_W:(j + 1) * GROUP_W] = raw[j][0, 0:CHUNK, :]
        tail_scr[:, j * GROUP_W:(j + 1) * GROUP_W] = raw[j][0, rows - 8:rows, :]

    def conv(c, j):
        w = cw_ref[:, j * GROUP_W:(j + 1) * GROUP_W]
        acc = None
        for d in range(4):
            if c == 0:
                x = ext_scr[pl.ds(8 - d, CHUNK), j * GROUP_W:(j + 1) * GROUP_W]
            else:
                x = raw[j][0, pl.ds(c * CHUNK - d, CHUNK), :]
            term = x * w[3 - d:4 - d, :]
            acc = term if acc is None else acc + term
        return _silu(acc)

    nch = rows // CHUNK
    pairs = [(c, h) for c in range(nch) for h in range(N_HEADS)]
    rr = lax.broadcasted_iota(jnp.int32, (N_HEADS * CHUNK, CHUNK), 0) & (CHUNK - 1)
    cc = lax.broadcasted_iota(jnp.int32, (N_HEADS * CHUNK, CHUNK), 1)
    tril = rr >= cc
    tril_f = tril.astype(F32)
    stril_f = (rr > cc).astype(F32)
    sub = lax.broadcasted_iota(jnp.int32, (1, GROUP_W), 1) & (HEAD_W - 1)
    lane_is = [(sub == i).astype(BF16) for i in range(6)]

    qc, kc, vc, qn, kn, cs = [], [], [], [], [], []
    for c in range(nch):
        qc.append(conv(c, 0))
        kc.append(conv(c, 1))
        vc.append(conv(c, 2))
        qn.append(_headsum(qc[c] * qc[c], gmat))
        kn.append(_headsum(kc[c] * kc[c], gmat))
        log_alpha = neg_a * _softplus(a_ref[0, pl.ds(c * CHUNK, CHUNK), :] + dtb)
        cs.append(_select_rows(ctab[0:2 * CHUNK], log_alpha))

    diff, prod, kb, beta = [], [], [], []
    for c in range(nch):
        qc[c] = qc[c] * lax.rsqrt(qn[c] + EPS) * (HEAD_W ** -0.5)
        kc[c] = kc[c] * lax.rsqrt(kn[c] + EPS)
        beta.append(_sigmoid(bl_ref[0, pl.ds(c * CHUNK, CHUNK), :]))
        kb.append(kc[c] * beta[c])
        g1, g2, g3 = _split3(cs[c][0:CHUNK])
        lhs = g1 * lane_is[0] + g2 * lane_is[1] + g3 * lane_is[2] + (lane_is[3] + lane_is[4] + lane_is[5])
        rhs = (lane_is[0] + lane_is[1] + lane_is[2]) - g1 * lane_is[3] - g2 * lane_is[4] - g3 * lane_is[5]
        diff.append(_dot_nt(_stack_heads(lhs, hm), rhs))
        prod.append(_dot_nt(jnp.concatenate([_stack_heads(_bf(qc[c]), hm), _stack_heads(_bf(kb[c]), hm)],
                                            axis=0), _bf(kc[c])))

    qg, kdec, vb, kbg, g_last, attn, m_mat = [], [], [], [], [], {}, {}
    for c in range(nch):
        gc = cs[c][0:CHUNK]
        rem = cs[c][CHUNK:2 * CHUNK]
        dec = jnp.exp(jnp.where(tril, diff[c], 0.0))
        attn_st = _bf(tril_f * dec * prod[c][0:N_HEADS * CHUNK])
        m_st = stril_f * dec * prod[c][N_HEADS * CHUNK:2 * N_HEADS * CHUNK]
        for h in range(N_HEADS):
            attn[c, h] = attn_st[h * CHUNK:(h + 1) * CHUNK]
            m_mat[c, h] = m_st[h * CHUNK:(h + 1) * CHUNK]
        qg.append(qc[c] * jnp.exp(gc))
        kdec.append(_bf(kc[c] * jnp.exp(rem)))
        vb.append(vc[c] * beta[c])
        kbg.append(kb[c] * jnp.exp(gc))
        g_last.append(jnp.exp(gc[CHUNK - 1:CHUNK]))

    t_inv = {p: eye - pair[N_LEVELS - 1] * m_mat[p] for p in pairs}
    for l in range(N_LEVELS - 2, -1, -1):
        t_b = {p: _bf(t_inv[p]) for p in pairs}
        tl = {p: _dot(t_b[p], _bf(pair[l] * m_mat[p])) for p in pairs}
        t_inv = {p: t_inv[p] - _dot(_bf(tl[p]), t_b[p]) for p in pairs}

    uw = []
    for c in range(nch):
        rhs = jnp.concatenate([vb[c], kbg[c]], axis=1)
        acc = None
        for h in range(N_HEADS):
            lhs = jnp.concatenate(_split2(t_inv[c, h]), axis=1)
            rhs_h = _bf(rhs * jnp.concatenate([hmf[h], hmf[h]], axis=1))
            term = _dot(lhs, jnp.concatenate([rhs_h, rhs_h], axis=0))
            acc = term if acc is None else acc + term
        uw.append(_bf(acc))

    o_lhs, o_add, s_lhs, s_add = [], [], [], []
    for c in range(nch):
        u_b = uw[c][:, 0:GROUP_W]
        w_b = uw[c][:, GROUP_W:2 * GROUP_W]
        aw = None
        for h in range(N_HEADS):
            term = _dot(attn[c, h], jnp.concatenate([u_b * hm[h], w_b * hm[h]], axis=1))
            aw = term if aw is None else aw + term
        o_add.append(aw[:, 0:GROUP_W])
        o_lhs.append(_bf(qg[c] - aw[:, GROUP_W:2 * GROUP_W]))
        ks = _dot_tn(kdec[c], uw[c])
        s_add.append(ks[:, 0:GROUP_W] * bd)
        s_lhs.append(_bf(-ks[:, GROUP_W:2 * GROUP_W] * bd))

    s = s_scr[...]
    for c in range(nch):
        sl = pl.ds(c * CHUNK, CHUNK)
        s_b = _bf(s)
        o = o_add[c] + _dot(o_lhs[c], s_b)
        s = s * g_last[c] + _dot(s_lhs[c], s_b) + s_add[c]
        y = _head_rmsnorm(o, gmat, ng) * _silu(g_ref[0, sl, :])
        o_ref[0, sl, :] = _bf(y)
    s_scr[...] = s


def _gdn_mixer(proj_b, conv_w, a_log_rep, dt_bias_rep, norm_g, ctab, gmat, bd, rows):
    bsz, seq, _ = proj_b.shape
    col = lambda c: pl.BlockSpec((1, rows, GROUP_W), lambda b, t, c=c: (b, t, c))
    whole = lambda a: pl.BlockSpec(a.shape, lambda b, t: (0,) * a.ndim)
    return pl.pallas_call(
        _gdn_body,
        grid=(bsz, seq // rows),
        in_specs=[col(0), col(1), col(2), col(3), col(4), col(5), whole(conv_w), whole(a_log_rep),
                  whole(dt_bias_rep), whole(norm_g), whole(ctab), whole(gmat), whole(bd)],
        out_specs=pl.BlockSpec((1, rows, GROUP_W), lambda b, t: (b, t, 0)),
        out_shape=jax.ShapeDtypeStruct((bsz, seq, GROUP_W), BF16),
        scratch_shapes=[pltpu.VMEM((GROUP_W, GROUP_W), F32),
                        pltpu.VMEM((8, 3 * GROUP_W), F32),
                        pltpu.VMEM((8 + CHUNK, 3 * GROUP_W), F32)],
        compiler_params=pltpu.CompilerParams(
            dimension_semantics=("parallel", "arbitrary"), vmem_limit_bytes=VMEM_LIMIT),
        name="gdn",
    )(proj_b, proj_b, proj_b, proj_b, proj_b, proj_b, conv_w, a_log_rep, dt_bias_rep, norm_g,
      ctab, gmat, bd)


def _sb_body(q_ref, k_ref, v_ref, ng_ref, later_ref, gmat_ref, o_ref):
    qi = pl.program_id(1)
    blk, unit = SB_BLOCK, SB_UNIT
    hm = _head_masks(GROUP_W, HEAD_W, BF16)
    later = later_ref[...]
    q = q_ref[0] * (HEAD_W ** -0.5)
    qs = jnp.concatenate([q * hm[h] for h in range(N_HEADS)], axis=0)
    t_loc = lax.broadcasted_iota(jnp.int32, (N_HEADS * blk, unit), 0) & (blk - 1)
    s_loc = lax.broadcasted_iota(jnp.int32, (N_HEADS * blk, unit), 1)

    def sweep(u, carry, acc, diagonal):
        kblk = k_ref[0, pl.ds(u * unit, unit), :]
        vblk = v_ref[0, pl.ds(u * unit, unit), :]
        z = _dot_nt(qs, kblk)
        l1p = jnp.log(1.0 + jnp.exp(-jnp.abs(z)))
        log_sig = jnp.minimum(z, 0.0) - l1p
        log_stay = -jnp.maximum(z, 0.0) - l1p
        if diagonal:
            past = (qi * blk + t_loc) > (u * unit + s_loc)
            log_stay = jnp.where(past, log_stay, 0.0)
        sfx = _dot(_bf(log_stay), later)
        wts = jnp.exp(log_sig + sfx + carry)
        if diagonal:
            wts = jnp.where(past, wts, 0.0)
        wts = _bf(wts)
        w_cat = jnp.concatenate([wts[h * blk:(h + 1) * blk] for h in range(N_HEADS)], axis=1)
        v_cat = jnp.concatenate([vblk * hm[h] for h in range(N_HEADS)], axis=0)
        acc = acc + _dot(w_cat, v_cat)
        return carry + jnp.sum(log_stay, axis=-1, keepdims=True), acc

    u0 = (qi * blk) // unit
    carry, acc = sweep(u0, jnp.zeros((N_HEADS * blk, 1), F32), jnp.zeros((blk, GROUP_W), F32), True)

    def live(state):
        return jnp.logical_and(state[0] < u0, state[1])

    def step(state):
        i, _, carry, acc = state
        carry, acc = sweep(u0 - 1 - i, carry, acc, False)
        return i + 1, jnp.max(carry) > SB_DEAD, carry, acc

    _, _, carry, acc = lax.while_loop(live, step, (0, jnp.max(carry) > SB_DEAD, carry, acc))
    o_ref[0] = _bf(_head_rmsnorm(acc, gmat_ref[...], ng_ref[...]))


def _sb_mixer(proj_c, norm_g, u2, gmat):
    bsz, seq, _ = proj_c.shape
    whole = lambda a: pl.BlockSpec(a.shape, lambda b, t: (0,) * a.ndim)
    return pl.pallas_call(
        _sb_body,
        grid=(bsz, seq // SB_BLOCK),
        in_specs=[pl.BlockSpec((1, SB_BLOCK, GROUP_W), lambda b, t: (b, t, 0)),
                  pl.BlockSpec((1, seq, GROUP_W), lambda b, t: (b, 0, 1)),
                  pl.BlockSpec((1, seq, GROUP_W), lambda b, t: (b, 0, 2)),
                  whole(norm_g), whole(u2), whole(gmat)],
        out_specs=pl.BlockSpec((1, SB_BLOCK, GROUP_W), lambda b, t: (b, t, 0)),
        out_shape=jax.ShapeDtypeStruct((bsz, seq, GROUP_W), BF16),
        compiler_params=pltpu.CompilerParams(
            dimension_semantics=("parallel", "arbitrary"), vmem_limit_bytes=VMEM_LIMIT),
        name="stickbreak",
    )(proj_c, proj_c, proj_c, norm_g, u2, gmat)


def _sb_table():
    j = np.arange(SB_UNIT)[:, None]
    s = np.arange(SB_UNIT)[None, :]
    return jnp.asarray((j > s).astype(np.float32), dtype=BF16)


def _ret_body(q_ref, k_ref, v_ref, g_ref, cos_ref, sin_ref, decay_ref, zeta_ref, xi_ref, gc_ref,
              bd_ref, ng_ref, gmat_ref, o_ref, s_scr):
    @pl.when(pl.program_id(1) == 0)
    def _():
        s_scr[...] = jnp.zeros_like(s_scr)

    chunk = decay_ref.shape[1]
    decay = decay_ref[...]
    lane = lax.broadcasted_iota(jnp.int32, (1, 2 * HEAD_W), 1)
    hq = [(((lane % 64) // 16) == h).astype(BF16) for h in range(N_HEADS)]
    hv = _head_masks(GROUP_W, HEAD_W, BF16)
    gmat = gmat_ref[...]
    ng = ng_ref[...]
    bd = bd_ref[...]
    zeta = zeta_ref[...]
    xi = xi_ref[...]
    gamma_c = gc_ref[...]

    nch = q_ref.shape[1] // chunk
    qx, o_intra, s_add = [], [], []
    for c in range(nch):
        sl = pl.ds(c * chunk, chunk)
        cos = cos_ref[sl, :]
        sin = sin_ref[sl, :]
        q = q_ref[0, sl, :]
        k = k_ref[0, sl, :]
        qr = q * cos + pltpu.roll(q, 64, axis=1) * sin
        kr = (k * cos + pltpu.roll(k, 64, axis=1) * sin) * (D_DK ** -0.5)
        v = _bf(v_ref[0, sl, :])
        sc = _bf(_dot_nt(_stack_heads(_bf(qr), hq), _bf(kr)) * decay)
        o = _dot(sc[0:chunk], v * hv[0])
        for h in range(1, N_HEADS):
            o = o + _dot(sc[h * chunk:(h + 1) * chunk], v * hv[h])
        o_intra.append(o)
        qx.append(_bf(qr * xi))
        s_add.append(_dot_tn(_bf(kr * zeta), v) * bd)

    s = s_scr[...]
    for c in range(nch):
        sl = pl.ds(c * chunk, chunk)
        o = o_intra[c] + _dot(qx[c], _bf(s))
        s = s * gamma_c + s_add[c]
        y = _head_rmsnorm(o, gmat, ng) * _silu(g_ref[0, sl, :])
        o_ref[0, sl, :] = _bf(y)
    s_scr[...] = s


def _ret_mixer(proj_d, tables, norm_g, gmat, rows):
    bsz, seq, _ = proj_d.shape
    cos, sin, decay, zeta, xi, gamma_c, bd = tables
    whole = lambda a: pl.BlockSpec(a.shape, lambda b, t: (0,) * a.ndim)
    return pl.pallas_call(
        _ret_body,
        grid=(bsz, seq // rows),
        in_specs=[pl.BlockSpec((1, rows, 128), lambda b, t: (b, t, 0)),
                  pl.BlockSpec((1, rows, 128), lambda b, t: (b, t, 1)),
                  pl.BlockSpec((1, rows, GROUP_W), lambda b, t: (b, t, 1)),
                  pl.BlockSpec((1, rows, GROUP_W), lambda b, t: (b, t, 2)),
                  pl.BlockSpec((rows, 128), lambda b, t: (t, 0)),
                  pl.BlockSpec((rows, 128), lambda b, t: (t, 0)),
                  whole(decay), whole(zeta), whole(xi), whole(gamma_c), whole(bd),
                  whole(norm_g), whole(gmat)],
        out_specs=pl.BlockSpec((1, rows, GROUP_W), lambda b, t: (b, t, 0)),
        out_shape=jax.ShapeDtypeStruct((bsz, seq, GROUP_W), BF16),
        scratch_shapes=[pltpu.VMEM((2 * HEAD_W, GROUP_W), F32)],
        compiler_params=pltpu.CompilerParams(
            dimension_semantics=("parallel", "arbitrary"), vmem_limit_bytes=VMEM_LIMIT),
        name="retention",
    )(proj_d, proj_d, proj_d, proj_d, cos, sin, decay, zeta, xi, gamma_c, bd, norm_g, gmat)


def _mix_weights(w_in):
    a_w = w_in[:, 0:1024]
    b0 = 1024
    b_main = w_in[:, b0:b0 + 1024]
    b_a = jnp.repeat(w_in[:, b0 + 1024:b0 + 1028], HEAD_W, axis=1)
    b_b = jnp.repeat(w_in[:, b0 + 1028:b0 + 1032], HEAD_W, axis=1)
    c0 = b0 + 1032
    c_w = w_in[:, c0:c0 + 768]
    d0 = c0 + 768
    perm = _ret_perm()
    d_q = w_in[:, d0:d0 + 128][:, perm]
    d_k = w_in[:, d0 + 128:d0 + 256][:, perm]
    d_rest = w_in[:, d0 + 256:d0 + 768]
    b_w = jnp.concatenate([b_main, b_a, b_b], axis=1)
    d_w = jnp.concatenate([d_q, d_k, d_rest], axis=1)
    return _bf(a_w), _bf(b_w), _bf(c_w), _bf(d_w)


def kernel(x, ffn1_norm, ffn1_w_in, ffn1_w_out, mix_norm, mix_w_in, mix_w_out, ffn2_norm, ffn2_w_in,
           ffn2_w_out, hgrn_lb_logits, hgrn_out_norm, gdn_conv_w, gdn_a_log, gdn_dt_bias, gdn_out_norm,
           sb_out_norm, ret_out_norm, final_norm):
    bsz, seq, d = x.shape
    depth = ffn1_norm.shape[0]
    n = bsz * seq
    d_ff = ffn1_w_out.shape[1]
    tm_ffn = 512
    tm = 256
    rows = MIXER_ROWS

    ctab = _cumsum_table()
    gmat = _head_block_matrix(GROUP_W, HEAD_W, GROUP_W, HEAD_W, BF16)
    bd = _head_block_matrix(GROUP_W, HEAD_W, GROUP_W, HEAD_W, F32)
    u2 = _sb_table()
    ret_tables = _retention_tables(seq, CHUNK)
    row = lambda a: a.reshape(1, -1).astype(F32)
    rep = lambda a: jnp.repeat(a.astype(F32), HEAD_W).reshape(1, GROUP_W)

    x = x.reshape(n, d)
    for l in range(depth):
        x = _ffn(x, row(ffn1_norm[l]), _bf(ffn1_w_in[l]), _bf(ffn1_w_out[l]), tm=tm_ffn)
        pa, pb, pc, pd = _mix_proj(x, row(mix_norm[l]), *_mix_weights(mix_w_in[l]), tm=tm)
        shp = lambda a: a.reshape(bsz, seq, a.shape[-1])
        ya = _hgrn_mixer(shp(pa), hgrn_lb_logits.astype(F32), row(hgrn_out_norm[l]), ctab, gmat, bd, l, rows)
        yb = _gdn_mixer(shp(pb), gdn_conv_w[l].astype(F32), rep(gdn_a_log[l]), rep(gdn_dt_bias[l]),
                        row(gdn_out_norm[l]), ctab, gmat, bd, rows)
        yc = _sb_mixer(shp(pc), row(sb_out_norm[l]), u2, gmat)
        yd = _ret_mixer(shp(pd), ret_tables, row(ret_out_norm[l]), gmat, rows)
        ys = [y.reshape(n, GROUP_W) for y in (ya, yb, yc, yd)]
        x = _ffn(x, row(ffn2_norm[l]), _bf(ffn2_w_in[l]), _bf(ffn2_w_out[l]), tm=tm_ffn,
                 mix=(ys, _bf(mix_w_out[l])),
                 final_gain=row(final_norm) if l == depth - 1 else None)
    return x.reshape(bsz, seq, d)
```

```python
import functools
import math

import numpy as np
import jax
import jax.numpy as jnp
from jax import lax
from jax.experimental import pallas as pl
from jax.experimental.pallas import tpu as pltpu

F32 = jnp.float32
BF16 = jnp.bfloat16

EPS = 1e-6
CHUNK = 64
N_HEADS = 4
HEAD_W = 64
GROUP_W = N_HEADS * HEAD_W
D_DK = 32
ROPE_BASE = 10000.0
N_LEVELS = 6
VMEM_LIMIT = 56 * 1024 * 1024

MIXER_ROWS = 512
GDN_GROUP = 4
FFN_COLS = 256
RET_CHUNK = 128
SB_BLOCK = 256
SB_UNIT = 256
SB_DEAD = -151.0
LOG2_E = 1.4426950408889634


def _bf(x):
    return x.astype(BF16)


def _dot(a, b):
    return jnp.dot(a, b, preferred_element_type=F32)


def _dot_nt(a, b):
    return lax.dot_general(a, b, (((1,), (1,)), ((), ())), preferred_element_type=F32)


def _dot_tn(a, b):
    return lax.dot_general(a, b, (((0,), (0,)), ((), ())), preferred_element_type=F32)


def _split2(x):
    hi = _bf(x)
    lo = _bf(x - hi.astype(F32))
    return hi, lo


def _split3(x):
    h1 = _bf(x)
    r = x - h1.astype(F32)
    h2 = _bf(r)
    h3 = _bf(r - h2.astype(F32))
    return h1, h2, h3


def _select_rows(w3, x):
    return _dot(w3, jnp.concatenate(_split3(x), axis=0))


def _headsum(x, gmat):
    hi, lo = _split2(x)
    return _dot(hi, gmat) + _dot(lo, gmat)


def _head_rmsnorm(o, gmat, gain):
    ms = _headsum(o * o, gmat) * (1.0 / HEAD_W)
    return o * lax.rsqrt(ms + EPS) * gain


def _sigmoid(x):
    return 1.0 / (1.0 + jnp.exp(-x))


def _silu(x):
    return x * _sigmoid(x)


def _softplus(x):
    return jnp.maximum(x, 0.0) + jnp.log(1.0 + jnp.exp(-jnp.abs(x)))


def _head_masks(width, lanes_per_head, dtype):
    lane = lax.broadcasted_iota(jnp.int32, (1, width), 1)
    return [((lane // lanes_per_head) == h).astype(dtype) for h in range(N_HEADS)]


def _stack_heads(x, masks):
    return jnp.concatenate([x * m for m in masks], axis=0)


def _level_masks(width, lane_stack=1):
    r = lax.broadcasted_iota(jnp.int32, (CHUNK, lane_stack * CHUNK), 0)
    c = lax.broadcasted_iota(jnp.int32, (CHUNK, lane_stack * CHUNK), 1) & (CHUNK - 1)
    rows = lax.broadcasted_iota(jnp.int32, (CHUNK, width), 0)
    pair, upper, lower = [], [], []
    for l in range(N_LEVELS):
        n = CHUNK >> l
        m = n // 2
        sh = N_LEVELS - l
        same = (r >> sh) == (c >> sh)
        pair.append((same & ((r & (n - 1)) >= m) & ((c & (n - 1)) < m)).astype(F32))
        upper.append((rows & (n - 1)) >= m)
        lower.append((rows & (n - 1)) < m)
    eye = (r == c).astype(F32)
    return pair, upper, lower, eye


def _decay_factors(cs, upper, lower):
    b = cs[0:CHUNK]
    out = []
    for l in range(N_LEVELS):
        ab = cs[(2 + l) * CHUNK:(3 + l) * CHUNK]
        fq = jnp.exp(jnp.where(upper[l], b - ab, 0.0))
        fk = jnp.exp(jnp.where(lower[l], ab - b, 0.0))
        out.append((fq, fk))
    return out


def _cumsum_table():
    t = np.arange(CHUNK)
    tri = (t[None, :] <= t[:, None]).astype(np.float32)
    up = (t[None, :] > t[:, None]).astype(np.float32)
    blocks = [tri, up]
    for l in range(N_LEVELS):
        n = CHUNK >> l
        anchor = (t // n) * n + n // 2 - 1
        blocks.append(tri[anchor])
    tab = np.concatenate(blocks, axis=0)
    return jnp.asarray(np.concatenate([tab, tab, tab], axis=1), dtype=BF16)


def _head_block_matrix(rows, rows_per_head, cols, cols_per_head, dtype):
    r = np.arange(rows)[:, None] // rows_per_head
    c = np.arange(cols)[None, :] // cols_per_head
    return jnp.asarray((r == c).astype(np.float32), dtype=dtype)


def _ret_lane_head(p):
    return (p % 64) // 16


def _retention_tables(seq, chunk):
    p = np.arange(128)
    half = D_DK // 2
    inv_freq = ROPE_BASE ** (-(p % 16).astype(np.float64) / half)
    ang = np.arange(seq, dtype=np.float64)[:, None] * inv_freq[None, :]
    cos = np.cos(ang)
    sin = np.sin(ang) * np.where(p < 64, -1.0, 1.0)[None, :]
    log_gamma = np.log(1.0 - 2.0 ** (-5.0 - np.arange(N_HEADS, dtype=np.float64)))
    c = np.arange(chunk, dtype=np.float64)
    rel = c[:, None] - c[None, :]
    decay = np.where(rel[None] >= 0, np.exp(rel[None] * log_gamma[:, None, None]), 0.0)
    decay = decay.transpose(1, 0, 2).reshape(chunk, N_HEADS * chunk)
    lane_h = _ret_lane_head(p)
    zeta = np.exp((chunk - 1 - c)[:, None] * log_gamma[lane_h][None, :])
    xi = np.exp((c + 1.0)[:, None] * log_gamma[lane_h][None, :])
    gamma_c = np.exp(chunk * log_gamma)[np.arange(GROUP_W) // HEAD_W][None, :]
    bd = (lane_h[:, None] == (np.arange(GROUP_W) // HEAD_W)[None, :]).astype(np.float32)
    f = lambda a: jnp.asarray(a, dtype=F32)
    return f(cos), f(sin), f(decay), f(zeta), f(xi), f(gamma_c), f(bd)


def _ret_perm():
    p = np.arange(128)
    h = _ret_lane_head(p)
    return h * D_DK + (p % 16) + np.where(p >= 64, 16, 0)


def _rms_rows(x, gain):
    return x * lax.rsqrt(jnp.mean(x * x, axis=-1, keepdims=True) + EPS) * gain


def _ffn_body(has_mix, has_final, *refs):
    refs = list(refs)
    x_ref = refs.pop(0)
    y_refs = [refs.pop(0) for _ in range(4)] if has_mix else []
    wmix_ref = refs.pop(0) if has_mix else None
    g_ref, win_ref, wout_ref = refs.pop(0), refs.pop(0), refs.pop(0)
    gf_ref = refs.pop(0) if has_final else None
    o_ref = refs.pop(0)

    x = x_ref[...]
    for m, y_ref in enumerate(y_refs):
        x = x + _dot(y_ref[...], wmix_ref[m * GROUP_W:(m + 1) * GROUP_W, :])
    h = _bf(_rms_rows(x, g_ref[...]))

    d_ff = wout_ref.shape[0]
    steps = d_ff // FFN_COLS
    acc = None
    act = None
    for c in range(steps + 1):
        if c < steps:
            gate = _dot(h, win_ref[:, c * FFN_COLS:(c + 1) * FFN_COLS])
            up = _dot(h, win_ref[:, d_ff + c * FFN_COLS:d_ff + (c + 1) * FFN_COLS])
        if act is not None:
            down = _dot(act, wout_ref[(c - 1) * FFN_COLS:c * FFN_COLS, :])
            acc = down if acc is None else acc + down
        if c < steps:
            act = _bf(_silu(gate) * up)
    y = x + 0.5 * acc
    if has_final:
        y = _rms_rows(y, gf_ref[...])
    o_ref[...] = y


def _layer_spec(a, layer):
    return pl.BlockSpec((None,) + a.shape[1:], lambda i: (layer, 0, 0), pipeline_mode=pl.Buffered(1))


def _ffn(x, layer, gain, w_in, w_out, *, tm, mix=None, final_gain=None):
    n, d = x.shape
    rows = lambda w: pl.BlockSpec((tm, w), lambda i: (i, 0))
    whole = lambda a: _layer_spec(a, layer) if a.ndim == 3 else pl.BlockSpec(a.shape, lambda i: (0, 0))
    args, specs = [x], [rows(d)]
    if mix is not None:
        ys, w_mix = mix
        args += list(ys) + [w_mix]
        specs += [rows(GROUP_W) for _ in ys] + [whole(w_mix)]
    args += [gain, w_in, w_out]
    specs += [whole(gain), whole(w_in), whole(w_out)]
    if final_gain is not None:
        args.append(final_gain)
        specs.append(whole(final_gain))
    return pl.pallas_call(
        functools.partial(_ffn_body, mix is not None, final_gain is not None),
        grid=(n // tm,),
        in_specs=specs,
        out_specs=rows(d),
        out_shape=jax.ShapeDtypeStruct((n, d), F32),
        compiler_params=pltpu.CompilerParams(
            dimension_semantics=("parallel",), vmem_limit_bytes=VMEM_LIMIT),
        name="ffn",
    )(*args)


def _proj_body(x_ref, g_ref, wa_ref, wb_ref, wc_ref, wd_ref, oa_ref, ob_ref, oc_ref, od_ref):
    h = _bf(_rms_rows(x_ref[...], g_ref[...]))
    oa_ref[...] = _dot(h, wa_ref[...])
    ob_ref[...] = _dot(h, wb_ref[...])
    oc_ref[...] = _bf(_dot(h, wc_ref[...]))
    od_ref[...] = _dot(h, wd_ref[...])


def _mix_proj(x, layer, gain, ws, *, tm):
    n, d = x.shape
    dts = (F32, F32, BF16, F32)
    return pl.pallas_call(
        _proj_body,
        grid=(n // tm,),
        in_specs=[pl.BlockSpec((tm, d), lambda i: (i, 0)), pl.BlockSpec((1, d), lambda i: (0, 0))]
        + [_layer_spec(w, layer) for w in ws],
        out_specs=[pl.BlockSpec((tm, w.shape[2]), lambda i: (i, 0)) for w in ws],
        out_shape=[jax.ShapeDtypeStruct((n, w.shape[2]), dt) for w, dt in zip(ws, dts)],
        compiler_params=pltpu.CompilerParams(
            dimension_semantics=("parallel",), vmem_limit_bytes=VMEM_LIMIT),
        name="mix_proj",
    )(x, gain, *ws)


def _hgrn_body(layer, q_ref, f_ref, i_ref, g_ref, lbl_ref, ng_ref, ctab_ref, gmat_ref, bd_ref,
               o_ref, st_scr):
    @pl.when(pl.program_id(1) == 0)
    def _():
        st_scr[...] = jnp.zeros_like(st_scr)

    logits = lbl_ref[...]
    e = jnp.exp(logits - jnp.max(logits, axis=0, keepdims=True))
    p = e / jnp.sum(e, axis=0, keepdims=True)
    lb = jnp.zeros_like(p[0:1])
    for r in range(1, layer + 1):
        lb = lb + p[r:r + 1]

    pair, upper, lower, eye = _level_masks(GROUP_W, lane_stack=N_HEADS)
    hm = _head_masks(GROUP_W, HEAD_W, BF16)
    ctab = ctab_ref[...]
    gmat = gmat_ref[...]
    bd = bd_ref[...]
    ng = ng_ref[...]

    chunks = range(q_ref.shape[1] // CHUNK)
    rows_of = lambda c: pl.ds(c * CHUNK, CHUNK)
    qs, kk, cs = [], [], []
    for c in chunks:
        f = lb + (1.0 - lb) * _sigmoid(f_ref[0, rows_of(c), :])
        qs.append(_silu(q_ref[0, rows_of(c), :]))
        kk.append(1.0 - f)
        cs.append(_select_rows(ctab, jnp.log(f)))

    scores = []
    for c in chunks:
        sc = eye * _dot_nt(_bf(qs[c]), _stack_heads(_bf(kk[c]), hm))
        for l, (fq, fk) in enumerate(_decay_factors(cs[c], upper, lower)):
            sc = sc + pair[l] * _dot_nt(_bf(qs[c] * fq), _stack_heads(_bf(kk[c] * fk), hm))
        scores.append(_bf(sc))

    qdec, b_last, o_intra, st_add = [], [], [], []
    for c in chunks:
        b = cs[c][0:CHUNK]
        rem = cs[c][CHUNK:2 * CHUNK]
        v = _bf(_silu(i_ref[0, rows_of(c), :]))
        o_intra.append(_dot(scores[c], _stack_heads(v, hm)))
        qdec.append(_bf(qs[c] * jnp.exp(b)))
        st_add.append(_dot_tn(v, _bf(kk[c] * jnp.exp(rem))) * bd)
        b_last.append(jnp.exp(b[CHUNK - 1:CHUNK]))

    st = st_scr[...]
    outs = []
    for c in chunks:
        outs.append(o_intra[c] + _dot_nt(qdec[c], _bf(st)))
        st = st * b_last[c] + st_add[c]
    st_scr[...] = st
    o_ref[0] = _bf(_head_rmsnorm(jnp.concatenate(outs, axis=0), gmat, ng) * _sigmoid(g_ref[0]))


def _hgrn_mixer(proj_a, lb_logits, norm_g, ctab, gmat, bd, layer, rows):
    bsz, seq, _ = proj_a.shape
    col = lambda c: pl.BlockSpec((1, rows, GROUP_W), lambda b, t, c=c: (b, t, c))
    whole = lambda a: pl.BlockSpec(a.shape, lambda b, t: (0,) * a.ndim)
    return pl.pallas_call(
        functools.partial(_hgrn_body, layer),
        grid=(bsz, seq // rows),
        in_specs=[col(0), col(1), col(2), col(3), whole(lb_logits), whole(norm_g),
                  whole(ctab), whole(gmat), whole(bd)],
        out_specs=pl.BlockSpec((1, rows, GROUP_W), lambda b, t: (b, t, 0)),
        out_shape=jax.ShapeDtypeStruct((bsz, seq, GROUP_W), BF16),
        scratch_shapes=[pltpu.VMEM((GROUP_W, GROUP_W), F32)],
        compiler_params=pltpu.CompilerParams(
            dimension_semantics=("parallel", "arbitrary"), vmem_limit_bytes=VMEM_LIMIT),
        name="hgrn2",
    )(proj_a, proj_a, proj_a, proj_a, lb_logits, norm_g, ctab, gmat, bd)


def _gdn_body(q_ref, k_ref, v_ref, g_ref, a_ref, bl_ref, cw_ref, alog_ref, dtb_ref, ng_ref,
              ctab_ref, gmat_ref, bd_ref, o_ref, s_scr, tail_scr, ext_scr):
    @pl.when(pl.program_id(1) == 0)
    def _():
        s_scr[...] = jnp.zeros_like(s_scr)
        tail_scr[...] = jnp.zeros_like(tail_scr)

    pair, upper, lower, eye = _level_masks(GROUP_W)
    hm = _head_masks(GROUP_W, HEAD_W, BF16)
    hmf = _head_masks(GROUP_W, HEAD_W, F32)
    ctab = ctab_ref[...]
    gmat = gmat_ref[...]
    bd = bd_ref[...]
    ng = ng_ref[...]
    neg_a = -jnp.exp(alog_ref[...])
    dtb = dtb_ref[...]
    raw = (q_ref, k_ref, v_ref)
    rows = q_ref.shape[1]

    for j in range(3):
        ext_scr[0:8, j * GROUP_W:(j + 1) * GROUP_W] = tail_scr[:, j * GROUP_W:(j + 1) * GROUP_W]
        ext_scr[8:8 + CHUNK, j * GROUP_W:(j + 1) * GROUP_W] = raw[j][0, 0:CHUNK, :]
        tail_scr[:, j * GROUP_W:(j + 1) * GROUP_W] = raw[j][0, rows - 8:rows, :]

    def conv(c, j):
        w = cw_ref[:, j * GROUP_W:(j + 1) * GROUP_W]
        acc = None
        for d in range(4):
            if c == 0:
                x = ext_scr[pl.ds(8 - d, CHUNK), j * GROUP_W:(j + 1) * GROUP_W]
            else:
                x = raw[j][0, pl.ds(c * CHUNK - d, CHUNK), :]
            term = x * w[3 - d:4 - d, :]
            acc = term if acc is None else acc + term
        return _silu(acc)

    nch = rows // CHUNK
    rr = lax.broadcasted_iota(jnp.int32, (N_HEADS * CHUNK, CHUNK), 0) & (CHUNK - 1)
    cc = lax.broadcasted_iota(jnp.int32, (N_HEADS * CHUNK, CHUNK), 1)
    tril = rr >= cc
    tril_f = tril.astype(F32)
    stril_f = (rr > cc).astype(F32)
    sub = lax.broadcasted_iota(jnp.int32, (1, GROUP_W), 1) & (HEAD_W - 1)
    lane_is = [(sub == i).astype(BF16) for i in range(6)]

    g_last, o_lhs, o_add, s_lhs, s_add = {}, {}, {}, {}, {}
    for grp in [range(g, min(g + GDN_GROUP, nch)) for g in range(0, nch, GDN_GROUP)]:
        pairs = [(c, h) for c in grp for h in range(N_HEADS)]

        qc, kc, vc, qn, kn, cs = {}, {}, {}, {}, {}, {}
        for c in grp:
            qc[c] = conv(c, 0)
            kc[c] = conv(c, 1)
            vc[c] = conv(c, 2)
            qn[c] = _headsum(qc[c] * qc[c], gmat)
            kn[c] = _headsum(kc[c] * kc[c], gmat)
            log_alpha = neg_a * _softplus(a_ref[0, pl.ds(c * CHUNK, CHUNK), :] + dtb)
            cs[c] = _select_rows(ctab[0:2 * CHUNK], log_alpha)

        diff, prod, kb, beta = {}, {}, {}, {}
        for c in grp:
            qc[c] = qc[c] * lax.rsqrt(qn[c] + EPS) * (HEAD_W ** -0.5)
            kc[c] = kc[c] * lax.rsqrt(kn[c] + EPS)
            beta[c] = _sigmoid(bl_ref[0, pl.ds(c * CHUNK, CHUNK), :])
            kb[c] = kc[c] * beta[c]
            g1, g2, g3 = _split3(cs[c][0:CHUNK])
            lhs = g1 * lane_is[0] + g2 * lane_is[1] + g3 * lane_is[2] + (lane_is[3] + lane_is[4] + lane_is[5])
            rhs = (lane_is[0] + lane_is[1] + lane_is[2]) - g1 * lane_is[3] - g2 * lane_is[4] - g3 * lane_is[5]
            diff[c] = _dot_nt(_stack_heads(lhs, hm), rhs)
            prod[c] = _dot_nt(jnp.concatenate([_stack_heads(_bf(qc[c]), hm), _stack_heads(_bf(kb[c]), hm)],
                                              axis=0), _bf(kc[c]))

        qg, kdec, vb, kbg, attn, m_mat = {}, {}, {}, {}, {}, {}
        for c in grp:
            gc = cs[c][0:CHUNK]
            rem = cs[c][CHUNK:2 * CHUNK]
            dec = jnp.exp(jnp.where(tril, diff[c], 0.0))
            attn_st = _bf(tril_f * dec * prod[c][0:N_HEADS * CHUNK])
            m_st = stril_f * dec * prod[c][N_HEADS * CHUNK:2 * N_HEADS * CHUNK]
            for h in range(N_HEADS):
                attn[c, h] = attn_st[h * CHUNK:(h + 1) * CHUNK]
                m_mat[c, h] = m_st[h * CHUNK:(h + 1) * CHUNK]
            qg[c] = qc[c] * jnp.exp(gc)
            kdec[c] = _bf(kc[c] * jnp.exp(rem))
            vb[c] = vc[c] * beta[c]
            kbg[c] = kb[c] * jnp.exp(gc)
            g_last[c] = jnp.exp(gc[CHUNK - 1:CHUNK])

        t_inv = {p: eye - pair[N_LEVELS - 1] * m_mat[p] for p in pairs}
        for l in range(N_LEVELS - 2, -1, -1):
            t_b = {p: _bf(t_inv[p]) for p in pairs}
            tl = {p: _dot(t_b[p], _bf(pair[l] * m_mat[p])) for p in pairs}
            t_inv = {p: t_inv[p] - _dot(_bf(tl[p]), t_b[p]) for p in pairs}

        uw = {}
        for c in grp:
            rhs = jnp.concatenate([vb[c], kbg[c]], axis=1)
            acc = None
            for h in range(N_HEADS):
                lhs = jnp.concatenate(_split2(t_inv[c, h]), axis=1)
                rhs_h = _bf(rhs * jnp.concatenate([hmf[h], hmf[h]], axis=1))
                term = _dot(lhs, jnp.concatenate([rhs_h, rhs_h], axis=0))
                acc = term if acc is None else acc + term
            uw[c] = _bf(acc)

        for c in grp:
            u_b = uw[c][:, 0:GROUP_W]
            w_b = uw[c][:, GROUP_W:2 * GROUP_W]
            aw = None
            for h in range(N_HEADS):
                term = _dot(attn[c, h], jnp.concatenate([u_b * hm[h], w_b * hm[h]], axis=1))
                aw = term if aw is None else aw + term
            o_add[c] = aw[:, 0:GROUP_W]
            o_lhs[c] = _bf(qg[c] - aw[:, GROUP_W:2 * GROUP_W])
            ks = _dot_tn(kdec[c], uw[c])
            s_add[c] = ks[:, 0:GROUP_W] * bd
            s_lhs[c] = _bf(-ks[:, GROUP_W:2 * GROUP_W] * bd)


    s = s_scr[...]
    outs = []
    for c in range(nch):
        s_b = _bf(s)
        outs.append(o_add[c] + _dot(o_lhs[c], s_b))
        s = s * g_last[c] + _dot(s_lhs[c], s_b) + s_add[c]
    s_scr[...] = s
    o_ref[0] = _bf(_head_rmsnorm(jnp.concatenate(outs, axis=0), gmat, ng) * _silu(g_ref[0]))


def _gdn_mixer(proj_b, conv_w, a_log_rep, dt_bias_rep, norm_g, ctab, gmat, bd, rows):
    bsz, seq, _ = proj_b.shape
    col = lambda c: pl.BlockSpec((1, rows, GROUP_W), lambda b, t, c=c: (b, t, c))
    whole = lambda a: pl.BlockSpec(a.shape, lambda b, t: (0,) * a.ndim)
    return pl.pallas_call(
        _gdn_body,
        grid=(bsz, seq // rows),
        in_specs=[col(0), col(1), col(2), col(3), col(4), col(5), whole(conv_w), whole(a_log_rep),
                  whole(dt_bias_rep), whole(norm_g), whole(ctab), whole(gmat), whole(bd)],
        out_specs=pl.BlockSpec((1, rows, GROUP_W), lambda b, t: (b, t, 0)),
        out_shape=jax.ShapeDtypeStruct((bsz, seq, GROUP_W), BF16),
        scratch_shapes=[pltpu.VMEM((GROUP_W, GROUP_W), F32),
                        pltpu.VMEM((8, 3 * GROUP_W), F32),
                        pltpu.VMEM((8 + CHUNK, 3 * GROUP_W), F32)],
        compiler_params=pltpu.CompilerParams(
            dimension_semantics=("parallel", "arbitrary"), vmem_limit_bytes=VMEM_LIMIT),
        name="gdn",
    )(proj_b, proj_b, proj_b, proj_b, proj_b, proj_b, conv_w, a_log_rep, dt_bias_rep, norm_g,
      ctab, gmat, bd)


def _sb_body(q_ref, k_ref, v_ref, ng_ref, later_ref, gmat_ref, o_ref):
    qi = pl.program_id(1)
    blk, unit = SB_BLOCK, SB_UNIT
    hm = _head_masks(GROUP_W, HEAD_W, BF16)
    later = later_ref[...]
    q = q_ref[0] * (HEAD_W ** -0.5)
    qs = jnp.concatenate([q * hm[h] for h in range(N_HEADS)], axis=0)
    t_loc = lax.broadcasted_iota(jnp.int32, (N_HEADS * blk, unit), 0) & (blk - 1)
    s_loc = lax.broadcasted_iota(jnp.int32, (N_HEADS * blk, unit), 1)

    def sweep(u, carry, acc, diagonal):
        kblk = k_ref[0, pl.ds(u * unit, unit), :]
        vblk = v_ref[0, pl.ds(u * unit, unit), :]
        z2 = _dot_nt(qs, kblk) * LOG2_E
        log_sig = jnp.minimum(z2, 0.0) - jnp.log2(1.0 + jnp.exp2(-jnp.abs(z2)))
        log_stay = log_sig - z2
        if diagonal:
            past = (qi * blk + t_loc) > (u * unit + s_loc)
            log_stay = jnp.where(past, log_stay, 0.0)
        sfx = _dot(_bf(log_stay), later)
        wts = jnp.exp2(log_sig + sfx + carry)
        if diagonal:
            wts = jnp.where(past, wts, 0.0)
        wts = _bf(wts)
        w_cat = jnp.concatenate([wts[h * blk:(h + 1) * blk] for h in range(N_HEADS)], axis=1)
        v_cat = jnp.concatenate([vblk * hm[h] for h in range(N_HEADS)], axis=0)
        acc = acc + _dot(w_cat, v_cat)
        return carry + jnp.sum(log_stay, axis=-1, keepdims=True), acc

    u0 = (qi * blk) // unit
    carry, acc = sweep(u0, jnp.zeros((N_HEADS * blk, 1), F32), jnp.zeros((blk, GROUP_W), F32), True)

    def live(state):
        return jnp.logical_and(state[0] < u0, state[1])

    def step(state):
        i, _, carry, acc = state
        carry, acc = sweep(u0 - 1 - i, carry, acc, False)
        return i + 1, jnp.max(carry) > SB_DEAD, carry, acc

    _, _, carry, acc = lax.while_loop(live, step, (0, jnp.max(carry) > SB_DEAD, carry, acc))
    o_ref[0] = _bf(_head_rmsnorm(acc, gmat_ref[...], ng_ref[...]))


def _sb_mixer(proj_c, norm_g, u2, gmat):
    bsz, seq, _ = proj_c.shape
    whole = lambda a: pl.BlockSpec(a.shape, lambda b, t: (0,) * a.ndim)
    return pl.pallas_call(
        _sb_body,
        grid=(bsz, seq // SB_BLOCK),
        in_specs=[pl.BlockSpec((1, SB_BLOCK, GROUP_W), lambda b, t: (b, t, 0)),
                  pl.BlockSpec((1, seq, GROUP_W), lambda b, t: (b, 0, 1)),
                  pl.BlockSpec((1, seq, GROUP_W), lambda b, t: (b, 0, 2)),
                  whole(norm_g), whole(u2), whole(gmat)],
        out_specs=pl.BlockSpec((1, SB_BLOCK, GROUP_W), lambda b, t: (b, t, 0)),
        out_shape=jax.ShapeDtypeStruct((bsz, seq, GROUP_W), BF16),
        compiler_params=pltpu.CompilerParams(
            dimension_semantics=("parallel", "arbitrary"), vmem_limit_bytes=VMEM_LIMIT),
        name="stickbreak",
    )(proj_c, proj_c, proj_c, norm_g, u2, gmat)


def _sb_table():
    j = np.arange(SB_UNIT)[:, None]
    s = np.arange(SB_UNIT)[None, :]
    return jnp.asarray((j > s).astype(np.float32), dtype=BF16)


def _ret_body(q_ref, k_ref, v_ref, g_ref, cos_ref, sin_ref, decay_ref, zeta_ref, xi_ref, gc_ref,
              bd_ref, ng_ref, gmat_ref, o_ref, s_scr):
    @pl.when(pl.program_id(1) == 0)
    def _():
        s_scr[...] = jnp.zeros_like(s_scr)

    chunk = decay_ref.shape[0]
    decay = decay_ref[...]
    lane = lax.broadcasted_iota(jnp.int32, (1, 2 * HEAD_W), 1)
    hq = [(((lane % 64) // 16) == h).astype(BF16) for h in range(N_HEADS)]
    hv = _head_masks(GROUP_W, HEAD_W, BF16)
    gmat = gmat_ref[...]
    ng = ng_ref[...]
    bd = bd_ref[...]
    zeta = zeta_ref[...]
    xi = xi_ref[...]
    gamma_c = gc_ref[...]

    nch = q_ref.shape[1] // chunk
    qx, o_intra, s_add = [], [], []
    for c in range(nch):
        sl = pl.ds(c * chunk, chunk)
        cos = cos_ref[sl, :]
        sin = sin_ref[sl, :]
        q = q_ref[0, sl, :]
        k = k_ref[0, sl, :]
        qr = q * cos + pltpu.roll(q, 64, axis=1) * sin
        kr = (k * cos + pltpu.roll(k, 64, axis=1) * sin) * (D_DK ** -0.5)
        v = _bf(v_ref[0, sl, :])
        sc = _bf(_dot_nt(_bf(qr), _stack_heads(_bf(kr), hq)) * decay)
        o_intra.append(_dot(sc, _stack_heads(v, hv)))
        qx.append(_bf(qr * xi))
        s_add.append(_dot_tn(_bf(kr * zeta), v) * bd)

    s = s_scr[...]
    outs = []
    for c in range(nch):
        outs.append(o_intra[c] + _dot(qx[c], _bf(s)))
        s = s * gamma_c + s_add[c]
    s_scr[...] = s
    o_ref[0] = _bf(_head_rmsnorm(jnp.concatenate(outs, axis=0), gmat, ng) * _silu(g_ref[0]))


def _ret_mixer(proj_d, tables, norm_g, gmat, rows):
    bsz, seq, _ = proj_d.shape
    cos, sin, decay, zeta, xi, gamma_c, bd = tables
    whole = lambda a: pl.BlockSpec(a.shape, lambda b, t: (0,) * a.ndim)
    return pl.pallas_call(
        _ret_body,
        grid=(bsz, seq // rows),
        in_specs=[pl.BlockSpec((1, rows, 128), lambda b, t: (b, t, 0)),
                  pl.BlockSpec((1, rows, 128), lambda b, t: (b, t, 1)),
                  pl.BlockSpec((1, rows, GROUP_W), lambda b, t: (b, t, 1)),
                  pl.BlockSpec((1, rows, GROUP_W), lambda b, t: (b, t, 2)),
                  pl.BlockSpec((rows, 128), lambda b, t: (t, 0)),
                  pl.BlockSpec((rows, 128), lambda b, t: (t, 0)),
                  whole(decay), whole(zeta), whole(xi), whole(gamma_c), whole(bd),
                  whole(norm_g), whole(gmat)],
        out_specs=pl.BlockSpec((1, rows, GROUP_W), lambda b, t: (b, t, 0)),
        out_shape=jax.ShapeDtypeStruct((bsz, seq, GROUP_W), BF16),
        scratch_shapes=[pltpu.VMEM((2 * HEAD_W, GROUP_W), F32)],
        compiler_params=pltpu.CompilerParams(
            dimension_semantics=("parallel", "arbitrary"), vmem_limit_bytes=VMEM_LIMIT),
        name="retention",
    )(proj_d, proj_d, proj_d, proj_d, cos, sin, decay, zeta, xi, gamma_c, bd, norm_g, gmat)


def _mix_weights(w_in):
    w_in = _bf(w_in)
    a_w = w_in[..., 0:1024]
    b0 = 1024
    b_main = w_in[..., b0:b0 + 1024]
    b_a = jnp.repeat(w_in[..., b0 + 1024:b0 + 1028], HEAD_W, axis=-1)
    b_b = jnp.repeat(w_in[..., b0 + 1028:b0 + 1032], HEAD_W, axis=-1)
    c0 = b0 + 1032
    c_w = w_in[..., c0:c0 + 768]
    d0 = c0 + 768
    perm = _ret_perm()
    d_q = w_in[..., d0:d0 + 128][..., perm]
    d_k = w_in[..., d0 + 128:d0 + 256][..., perm]
    d_rest = w_in[..., d0 + 256:d0 + 768]
    b_w = jnp.concatenate([b_main, b_a, b_b], axis=-1)
    d_w = jnp.concatenate([d_q, d_k, d_rest], axis=-1)
    return a_w, b_w, c_w, d_w


def kernel(x, ffn1_norm, ffn1_w_in, ffn1_w_out, mix_norm, mix_w_in, mix_w_out, ffn2_norm, ffn2_w_in,
           ffn2_w_out, hgrn_lb_logits, hgrn_out_norm, gdn_conv_w, gdn_a_log, gdn_dt_bias, gdn_out_norm,
           sb_out_norm, ret_out_norm, final_norm):
    bsz, seq, d = x.shape
    depth = ffn1_norm.shape[0]
    n = bsz * seq
    d_ff = ffn1_w_out.shape[1]
    tm_ffn = 512
    tm = 256
    rows = MIXER_ROWS

    ctab = _cumsum_table()
    gmat = _head_block_matrix(GROUP_W, HEAD_W, GROUP_W, HEAD_W, BF16)
    bd = _head_block_matrix(GROUP_W, HEAD_W, GROUP_W, HEAD_W, F32)
    u2 = _sb_table()
    ret_tables = _retention_tables(seq, RET_CHUNK)
    row = lambda a: a.reshape(1, -1).astype(F32)
    rep = lambda a: jnp.repeat(a.astype(F32), HEAD_W).reshape(1, GROUP_W)

    w1_in, w1_out, w2_in, w2_out, w_mix_out = (_bf(w) for w in (ffn1_w_in, ffn1_w_out, ffn2_w_in, ffn2_w_out,
                                                                  mix_w_out))
    w_mix_in = _mix_weights(mix_w_in)

    x = x.reshape(n, d)
    for l in range(depth):
        x = _ffn(x, l, row(ffn1_norm[l]), w1_in, w1_out, tm=tm_ffn)
        pa, pb, pc, pd = _mix_proj(x, l, row(mix_norm[l]), w_mix_in, tm=tm)
        shp = lambda a: a.reshape(bsz, seq, a.shape[-1])
        ya = _hgrn_mixer(shp(pa), hgrn_lb_logits.astype(F32), row(hgrn_out_norm[l]), ctab, gmat, bd, l, rows)
        yb = _gdn_mixer(shp(pb), gdn_conv_w[l].astype(F32), rep(gdn_a_log[l]), rep(gdn_dt_bias[l]),
                        row(gdn_out_norm[l]), ctab, gmat, bd, rows)
        yc = _sb_mixer(shp(pc), row(sb_out_norm[l]), u2, gmat)
        yd = _ret_mixer(shp(pd), ret_tables, row(ret_out_norm[l]), gmat, rows)
        ys = [y.reshape(n, GROUP_W) for y in (ya, yb, yc, yd)]
        x = _ffn(x, l, row(ffn2_norm[l]), w2_in, w2_out, tm=tm_ffn, mix=(ys, w_mix_out),
                 final_gain=row(final_norm) if l == depth - 1 else None)
    return x.reshape(bsz, seq, d)
```

```python
import functools
import math

import numpy as np
import jax
import jax.numpy as jnp
from jax import lax
from jax.experimental import pallas as pl
from jax.experimental.pallas import tpu as pltpu

F32 = jnp.float32
BF16 = jnp.bfloat16

EPS = 1e-6
CHUNK = 64
N_HEADS = 4
HEAD_W = 64
GROUP_W = N_HEADS * HEAD_W
D_DK = 32
ROPE_BASE = 10000.0
N_LEVELS = 6
VMEM_LIMIT = 56 * 1024 * 1024

MIXER_ROWS = 512
GDN_ROWS = 512
GDN_GROUP = 4
FFN_COLS = 256
RET_CHUNK = 128
SB_BLOCK = 256
SB_UNIT = 256
SB_DEAD = -151.0
LOG2_E = 1.4426950408889634


def _bf(x):
    return x.astype(BF16)


def _dot(a, b):
    return jnp.dot(a, b, preferred_element_type=F32)


def _dot_nt(a, b):
    return lax.dot_general(a, b, (((1,), (1,)), ((), ())), preferred_element_type=F32)


def _dot_tn(a, b):
    return lax.dot_general(a, b, (((0,), (0,)), ((), ())), preferred_element_type=F32)


def _split2(x):
    hi = _bf(x)
    lo = _bf(x - hi.astype(F32))
    return hi, lo


def _split3(x):
    h1 = _bf(x)
    r = x - h1.astype(F32)
    h2 = _bf(r)
    h3 = _bf(r - h2.astype(F32))
    return h1, h2, h3


def _select_rows(w3, x):
    return _dot(w3, jnp.concatenate(_split3(x), axis=0))


def _headsum(x, gmat):
    hi, lo = _split2(x)
    return _dot(hi, gmat) + _dot(lo, gmat)


def _head_rmsnorm(o, gmat, gain):
    ms = _headsum(o * o, gmat) * (1.0 / HEAD_W)
    return o * lax.rsqrt(ms + EPS) * gain


def _sigmoid(x):
    return 0.5 * jnp.tanh(0.5 * x) + 0.5


def _sigmoid_rel(x):
    return 1.0 / (1.0 + jnp.exp(-x))


def _silu(x):
    return x * _sigmoid(x)


def _softplus(x):
    return jnp.maximum(x, 0.0) + jnp.log(1.0 + jnp.exp(-jnp.abs(x)))


def _head_masks(width, lanes_per_head, dtype):
    lane = lax.broadcasted_iota(jnp.int32, (1, width), 1)
    return [((lane // lanes_per_head) == h).astype(dtype) for h in range(N_HEADS)]


def _stack_heads(x, masks):
    return jnp.concatenate([x * m for m in masks], axis=0)


def _level_masks(lane_stack=1):
    r = lax.broadcasted_iota(jnp.int32, (CHUNK, lane_stack * CHUNK), 0)
    c = lax.broadcasted_iota(jnp.int32, (CHUNK, lane_stack * CHUNK), 1) & (CHUNK - 1)
    pair = []
    for l in range(N_LEVELS):
        n = CHUNK >> l
        m = n // 2
        sh = N_LEVELS - l
        same = (r >> sh) == (c >> sh)
        pair.append((same & ((r & (n - 1)) >= m) & ((c & (n - 1)) < m)).astype(F32))
    eye = (r == c).astype(F32)
    return pair, eye


def _decay_factors(cs):
    return [(jnp.exp(cs[(2 + l) * CHUNK:(3 + l) * CHUNK]),
             jnp.exp(cs[(2 + N_LEVELS + l) * CHUNK:(3 + N_LEVELS + l) * CHUNK])) for l in range(N_LEVELS)]


def _cumsum_table():
    t = np.arange(CHUNK)
    tri = (t[None, :] <= t[:, None]).astype(np.float32)
    up = (t[None, :] > t[:, None]).astype(np.float32)
    q_blocks, k_blocks = [], []
    for l in range(N_LEVELS):
        n = CHUNK >> l
        m = n // 2
        anchor = (t // n) * n + m - 1
        upper = ((t % n) >= m)[:, None]
        q_blocks.append(np.where(upper, tri - tri[anchor], 0.0))
        k_blocks.append(np.where(~upper, tri[anchor] - tri, 0.0))
    tab = np.concatenate([tri, up] + q_blocks + k_blocks, axis=0)
    assert set(np.unique(tab)) <= {0.0, 1.0}
    return jnp.asarray(np.concatenate([tab, tab, tab], axis=1), dtype=BF16)


def _head_block_matrix(rows, rows_per_head, cols, cols_per_head, dtype):
    r = np.arange(rows)[:, None] // rows_per_head
    c = np.arange(cols)[None, :] // cols_per_head
    return jnp.asarray((r == c).astype(np.float32), dtype=dtype)


def _ret_lane_head(p):
    return (p % 64) // 16


def _retention_tables(seq, chunk):
    p = np.arange(128)
    half = D_DK // 2
    inv_freq = ROPE_BASE ** (-(p % 16).astype(np.float64) / half)
    ang = np.arange(seq, dtype=np.float64)[:, None] * inv_freq[None, :]
    cos = np.cos(ang)
    sin = np.sin(ang) * np.where(p < 64, -1.0, 1.0)[None, :]
    log_gamma = np.log(1.0 - 2.0 ** (-5.0 - np.arange(N_HEADS, dtype=np.float64)))
    c = np.arange(chunk, dtype=np.float64)
    rel = c[:, None] - c[None, :]
    decay = np.where(rel[None] >= 0, np.exp(rel[None] * log_gamma[:, None, None]), 0.0)
    decay = decay.transpose(1, 0, 2).reshape(chunk, N_HEADS * chunk)
    lane_h = _ret_lane_head(p)
    zeta = np.exp((chunk - 1 - c)[:, None] * log_gamma[lane_h][None, :])
    xi = np.exp((c + 1.0)[:, None] * log_gamma[lane_h][None, :])
    gamma_c = np.exp(chunk * log_gamma)[np.arange(GROUP_W) // HEAD_W][None, :]
    bd = (lane_h[:, None] == (np.arange(GROUP_W) // HEAD_W)[None, :]).astype(np.float32)
    f = lambda a: jnp.asarray(a, dtype=F32)
    return f(cos), f(sin), f(decay), f(zeta), f(xi), f(gamma_c), f(bd)


def _ret_perm():
    p = np.arange(128)
    h = _ret_lane_head(p)
    return h * D_DK + (p % 16) + np.where(p >= 64, 16, 0)


def _rms_rows(x, gain):
    return x * lax.rsqrt(jnp.mean(x * x, axis=-1, keepdims=True) + EPS) * gain


def _ffn_body(has_mix, has_final, *refs):
    refs = list(refs)
    x_ref = refs.pop(0)
    y_refs = [refs.pop(0) for _ in range(4)] if has_mix else []
    wmix_ref = refs.pop(0) if has_mix else None
    g_ref, win_ref, wout_ref = refs.pop(0), refs.pop(0), refs.pop(0)
    gf_ref = refs.pop(0) if has_final else None
    o_ref = refs.pop(0)

    x = x_ref[...]
    for m, y_ref in enumerate(y_refs):
        x = x + _dot(y_ref[...], wmix_ref[m * GROUP_W:(m + 1) * GROUP_W, :])
    h = _bf(_rms_rows(x, g_ref[...]))

    d_ff = wout_ref.shape[0]
    steps = d_ff // FFN_COLS
    acc = None
    act = None
    for c in range(steps + 1):
        if c < steps:
            gate = _dot(h, win_ref[:, c * FFN_COLS:(c + 1) * FFN_COLS])
            up = _dot(h, win_ref[:, d_ff + c * FFN_COLS:d_ff + (c + 1) * FFN_COLS])
        if act is not None:
            down = _dot(act, wout_ref[(c - 1) * FFN_COLS:c * FFN_COLS, :])
            acc = down if acc is None else acc + down
        if c < steps:
            act = _bf(_silu(gate) * up)
    y = x + 0.5 * acc
    if has_final:
        y = _rms_rows(y, gf_ref[...])
    o_ref[...] = y


def _layer_spec(a, layer):
    return pl.BlockSpec((None,) + a.shape[1:], lambda i: (layer, 0, 0), pipeline_mode=pl.Buffered(1))


def _ffn(x, layer, gain, w_in, w_out, *, tm, mix=None, final_gain=None):
    n, d = x.shape
    rows = lambda w: pl.BlockSpec((tm, w), lambda i: (i, 0))
    whole = lambda a: _layer_spec(a, layer) if a.ndim == 3 else pl.BlockSpec(a.shape, lambda i: (0, 0))
    args, specs = [x], [rows(d)]
    if mix is not None:
        ys, w_mix = mix
        args += list(ys) + [w_mix]
        specs += [rows(GROUP_W) for _ in ys] + [whole(w_mix)]
    args += [gain, w_in, w_out]
    specs += [whole(gain), whole(w_in), whole(w_out)]
    if final_gain is not None:
        args.append(final_gain)
        specs.append(whole(final_gain))
    return pl.pallas_call(
        functools.partial(_ffn_body, mix is not None, final_gain is not None),
        grid=(n // tm,),
        in_specs=specs,
        out_specs=rows(d),
        out_shape=jax.ShapeDtypeStruct((n, d), F32),
        compiler_params=pltpu.CompilerParams(
            dimension_semantics=("parallel",), vmem_limit_bytes=VMEM_LIMIT),
        name="ffn",
    )(*args)


def _proj_body(x_ref, g_ref, wa_ref, wb_ref, wc_ref, wd_ref, oa_ref, ob_ref, oc_ref, od_ref):
    h = _bf(_rms_rows(x_ref[...], g_ref[...]))
    oa_ref[...] = _dot(h, wa_ref[...])
    ob_ref[...] = _dot(h, wb_ref[...])
    oc_ref[...] = _bf(_dot(h, wc_ref[...]))
    od_ref[...] = _dot(h, wd_ref[...])


def _mix_proj(x, layer, gain, ws, *, tm):
    n, d = x.shape
    dts = (F32, F32, BF16, F32)
    return pl.pallas_call(
        _proj_body,
        grid=(n // tm,),
        in_specs=[pl.BlockSpec((tm, d), lambda i: (i, 0)), pl.BlockSpec((1, d), lambda i: (0, 0))]
        + [_layer_spec(w, layer) for w in ws],
        out_specs=[pl.BlockSpec((tm, w.shape[2]), lambda i: (i, 0)) for w in ws],
        out_shape=[jax.ShapeDtypeStruct((n, w.shape[2]), dt) for w, dt in zip(ws, dts)],
        compiler_params=pltpu.CompilerParams(
            dimension_semantics=("parallel",), vmem_limit_bytes=VMEM_LIMIT),
        name="mix_proj",
    )(x, gain, *ws)


def _hgrn_body(layer, q_ref, f_ref, i_ref, g_ref, lbl_ref, ng_ref, ctab_ref, gmat_ref, bd_ref,
               o_ref, st_scr):
    @pl.when(pl.program_id(1) == 0)
    def _():
        st_scr[...] = jnp.zeros_like(st_scr)

    logits = lbl_ref[...]
    e = jnp.exp(logits - jnp.max(logits, axis=0, keepdims=True))
    p = e / jnp.sum(e, axis=0, keepdims=True)
    lb = jnp.zeros_like(p[0:1])
    for r in range(1, layer + 1):
        lb = lb + p[r:r + 1]

    pair, eye = _level_masks(lane_stack=N_HEADS)
    hm = _head_masks(GROUP_W, HEAD_W, BF16)
    ctab = ctab_ref[...]
    gmat = gmat_ref[...]
    bd = bd_ref[...]
    ng = ng_ref[...]

    chunks = range(q_ref.shape[1] // CHUNK)
    rows_of = lambda c: pl.ds(c * CHUNK, CHUNK)
    qs, kk, cs = [], [], []
    for c in chunks:
        f = lb + (1.0 - lb) * _sigmoid_rel(f_ref[0, rows_of(c), :])
        qs.append(_silu(q_ref[0, rows_of(c), :]))
        kk.append(1.0 - f)
        cs.append(_select_rows(ctab, jnp.log(f)))

    scores = []
    for c in chunks:
        sc = eye * _dot_nt(_bf(qs[c]), _stack_heads(_bf(kk[c]), hm))
        for l, (fq, fk) in enumerate(_decay_factors(cs[c])):
            sc = sc + pair[l] * _dot_nt(_bf(qs[c] * fq), _stack_heads(_bf(kk[c] * fk), hm))
        scores.append(_bf(sc))

    qdec, b_last, o_intra, st_add = [], [], [], []
    for c in chunks:
        b = cs[c][0:CHUNK]
        rem = cs[c][CHUNK:2 * CHUNK]
        v = _bf(_silu(i_ref[0, rows_of(c), :]))
        o_intra.append(_dot(scores[c], _stack_heads(v, hm)))
        qdec.append(_bf(qs[c] * jnp.exp(b)))
        st_add.append(_dot_tn(v, _bf(kk[c] * jnp.exp(rem))) * bd)
        b_last.append(jnp.exp(b[CHUNK - 1:CHUNK]))

    st = st_scr[...]
    outs = []
    for c in chunks:
        outs.append(o_intra[c] + _dot_nt(qdec[c], _bf(st)))
        st = st * b_last[c] + st_add[c]
    st_scr[...] = st
    o_ref[0] = _bf(_head_rmsnorm(jnp.concatenate(outs, axis=0), gmat, ng) * _sigmoid(g_ref[0]))


def _hgrn_mixer(proj_a, lb_logits, norm_g, ctab, gmat, bd, layer, rows):
    bsz, seq, _ = proj_a.shape
    col = lambda c: pl.BlockSpec((1, rows, GROUP_W), lambda b, t, c=c: (b, t, c))
    whole = lambda a: pl.BlockSpec(a.shape, lambda b, t: (0,) * a.ndim)
    return pl.pallas_call(
        functools.partial(_hgrn_body, layer),
        grid=(bsz, seq // rows),
        in_specs=[col(0), col(1), col(2), col(3), whole(lb_logits), whole(norm_g),
                  whole(ctab), whole(gmat), whole(bd)],
        out_specs=pl.BlockSpec((1, rows, GROUP_W), lambda b, t: (b, t, 0)),
        out_shape=jax.ShapeDtypeStruct((bsz, seq, GROUP_W), BF16),
        scratch_shapes=[pltpu.VMEM((GROUP_W, GROUP_W), F32)],
        compiler_params=pltpu.CompilerParams(
            dimension_semantics=("parallel", "arbitrary"), vmem_limit_bytes=VMEM_LIMIT),
        name="hgrn2",
    )(proj_a, proj_a, proj_a, proj_a, lb_logits, norm_g, ctab, gmat, bd)


def _gdn_body(q_ref, k_ref, v_ref, g_ref, a_ref, bl_ref, cw_ref, alog_ref, dtb_ref, ng_ref,
              ctab_ref, gmat_ref, bd_ref, o_ref, s_scr, tail_scr, ext_scr):
    @pl.when(pl.program_id(1) == 0)
    def _():
        s_scr[...] = jnp.zeros_like(s_scr)
        tail_scr[...] = jnp.zeros_like(tail_scr)

    pair, eye = _level_masks()
    hm = _head_masks(GROUP_W, HEAD_W, BF16)
    hmf = _head_masks(GROUP_W, HEAD_W, F32)
    ctab = ctab_ref[...]
    gmat = gmat_ref[...]
    bd = bd_ref[...]
    ng = ng_ref[...]
    neg_a = -jnp.exp(alog_ref[...])
    dtb = dtb_ref[...]
    raw = (q_ref, k_ref, v_ref)
    rows = q_ref.shape[1]

    for j in range(3):
        ext_scr[0:8, j * GROUP_W:(j + 1) * GROUP_W] = tail_scr[:, j * GROUP_W:(j + 1) * GROUP_W]
        ext_scr[8:8 + CHUNK, j * GROUP_W:(j + 1) * GROUP_W] = raw[j][0, 0:CHUNK, :]
        tail_scr[:, j * GROUP_W:(j + 1) * GROUP_W] = raw[j][0, rows - 8:rows, :]

    def conv(c, j):
        w = cw_ref[:, j * GROUP_W:(j + 1) * GROUP_W]
        acc = None
        for d in range(4):
            if c == 0:
                x = ext_scr[pl.ds(8 - d, CHUNK), j * GROUP_W:(j + 1) * GROUP_W]
            else:
                x = raw[j][0, pl.ds(c * CHUNK - d, CHUNK), :]
            term = x * w[3 - d:4 - d, :]
            acc = term if acc is None else acc + term
        return _silu(acc)

    nch = rows // CHUNK
    rr = lax.broadcasted_iota(jnp.int32, (N_HEADS * CHUNK, CHUNK), 0) & (CHUNK - 1)
    cc = lax.broadcasted_iota(jnp.int32, (N_HEADS * CHUNK, CHUNK), 1)
    tril = rr >= cc
    tril_f = tril.astype(F32)
    stril_f = (rr > cc).astype(F32)
    sub = lax.broadcasted_iota(jnp.int32, (1, GROUP_W), 1) & (HEAD_W - 1)
    lane_is = [(sub == i).astype(BF16) for i in range(6)]
    qc, kc, vc, qn, kn, cs, diff, prod, kb, beta = ({} for _ in range(10))
    qg, kdec, vb, kbg, g_last, attn, m_mat, t_inv = ({} for _ in range(8))
    o_lhs, o_add, s_lhs, s_add = {}, {}, {}, {}

    def phase1(grp):
        for c in grp:
            qc[c] = conv(c, 0)
            kc[c] = conv(c, 1)
            vc[c] = conv(c, 2)
            qn[c] = _headsum(qc[c] * qc[c], gmat)
            kn[c] = _headsum(kc[c] * kc[c], gmat)
            log_alpha = neg_a * _softplus(a_ref[0, pl.ds(c * CHUNK, CHUNK), :] + dtb)
            cs[c] = _select_rows(ctab[0:2 * CHUNK], log_alpha)
            yield
        for c in grp:
            qc[c] = qc[c] * lax.rsqrt(qn[c] + EPS) * (HEAD_W ** -0.5)
            kc[c] = kc[c] * lax.rsqrt(kn[c] + EPS)
            beta[c] = _sigmoid(bl_ref[0, pl.ds(c * CHUNK, CHUNK), :])
            kb[c] = kc[c] * beta[c]
            g1, g2, g3 = _split3(cs[c][0:CHUNK])
            lhs = g1 * lane_is[0] + g2 * lane_is[1] + g3 * lane_is[2] + (lane_is[3] + lane_is[4] + lane_is[5])
            rhs = (lane_is[0] + lane_is[1] + lane_is[2]) - g1 * lane_is[3] - g2 * lane_is[4] - g3 * lane_is[5]
            diff[c] = _dot_nt(_stack_heads(lhs, hm), rhs)
            prod[c] = _dot_nt(jnp.concatenate([_stack_heads(_bf(qc[c]), hm), _stack_heads(_bf(kb[c]), hm)],
                                              axis=0), _bf(kc[c]))
            yield
        for c in grp:
            gc = cs[c][0:CHUNK]
            rem = cs[c][CHUNK:2 * CHUNK]
            dec = jnp.exp(jnp.where(tril, diff[c], 0.0))
            attn_st = _bf(tril_f * dec * prod[c][0:N_HEADS * CHUNK])
            m_st = stril_f * dec * prod[c][N_HEADS * CHUNK:2 * N_HEADS * CHUNK]
            for h in range(N_HEADS):
                attn[c, h] = attn_st[h * CHUNK:(h + 1) * CHUNK]
                m_mat[c, h] = m_st[h * CHUNK:(h + 1) * CHUNK]
            qg[c] = qc[c] * jnp.exp(gc)
            kdec[c] = _bf(kc[c] * jnp.exp(rem))
            vb[c] = vc[c] * beta[c]
            kbg[c] = kb[c] * jnp.exp(gc)
            g_last[c] = jnp.exp(gc[CHUNK - 1:CHUNK])
            yield

    def phase2(grp):
        pairs = [(c, h) for c in grp for h in range(N_HEADS)]
        for p in pairs:
            t_inv[p] = eye - pair[N_LEVELS - 1] * m_mat[p]
        for l in range(N_LEVELS - 2, -1, -1):
            t_b = {p: _bf(t_inv[p]) for p in pairs}
            tl = {p: _dot(t_b[p], _bf(pair[l] * m_mat[p])) for p in pairs}
            yield
            for p in pairs:
                t_inv[p] = t_inv[p] - _dot(_bf(tl[p]), t_b[p])
            yield

    def phase3(grp):
        uw = {}
        for c in grp:
            rhs = jnp.concatenate([vb[c], kbg[c]], axis=1)
            acc = None
            for h in range(N_HEADS):
                lhs = jnp.concatenate(_split2(t_inv[c, h]), axis=1)
                rhs_h = _bf(rhs * jnp.concatenate([hmf[h], hmf[h]], axis=1))
                term = _dot(lhs, jnp.concatenate([rhs_h, rhs_h], axis=0))
                acc = term if acc is None else acc + term
            uw[c] = _bf(acc)
        for c in grp:
            u_b = uw[c][:, 0:GROUP_W]
            w_b = uw[c][:, GROUP_W:2 * GROUP_W]
            aw = None
            for h in range(N_HEADS):
                term = _dot(attn[c, h], jnp.concatenate([u_b * hm[h], w_b * hm[h]], axis=1))
                aw = term if aw is None else aw + term
            o_add[c] = aw[:, 0:GROUP_W]
            o_lhs[c] = _bf(qg[c] - aw[:, GROUP_W:2 * GROUP_W])
            ks = _dot_tn(kdec[c], uw[c])
            s_add[c] = ks[:, 0:GROUP_W] * bd
            s_lhs[c] = _bf(-ks[:, GROUP_W:2 * GROUP_W] * bd)

    groups = [range(g, min(g + GDN_GROUP, nch)) for g in range(0, nch, GDN_GROUP)]
    for _ in phase1(groups[0]):
        pass
    for gi, grp in enumerate(groups):
        ahead = phase1(groups[gi + 1]) if gi + 1 < len(groups) else iter(())
        for _ in phase2(grp):
            next(ahead, None)
        for _ in ahead:
            pass
        phase3(grp)

    s = s_scr[...]
    outs = []
    for c in range(nch):
        s_b = _bf(s)
        outs.append(o_add[c] + _dot(o_lhs[c], s_b))
        s = s * g_last[c] + _dot(s_lhs[c], s_b) + s_add[c]
    s_scr[...] = s
    o_ref[0] = _bf(_head_rmsnorm(jnp.concatenate(outs, axis=0), gmat, ng) * _silu(g_ref[0]))


def _gdn_mixer(proj_b, conv_w, a_log_rep, dt_bias_rep, norm_g, ctab, gmat, bd, rows):
    bsz, seq, _ = proj_b.shape
    col = lambda c: pl.BlockSpec((1, rows, GROUP_W), lambda b, t, c=c: (b, t, c))
    whole = lambda a: pl.BlockSpec(a.shape, lambda b, t: (0,) * a.ndim)
    return pl.pallas_call(
        _gdn_body,
        grid=(bsz, seq // rows),
        in_specs=[col(0), col(1), col(2), col(3), col(4), col(5), whole(conv_w), whole(a_log_rep),
                  whole(dt_bias_rep), whole(norm_g), whole(ctab), whole(gmat), whole(bd)],
        out_specs=pl.BlockSpec((1, rows, GROUP_W), lambda b, t: (b, t, 0)),
        out_shape=jax.ShapeDtypeStruct((bsz, seq, GROUP_W), BF16),
        scratch_shapes=[pltpu.VMEM((GROUP_W, GROUP_W), F32),
                        pltpu.VMEM((8, 3 * GROUP_W), F32),
                        pltpu.VMEM((8 + CHUNK, 3 * GROUP_W), F32)],
        compiler_params=pltpu.CompilerParams(
            dimension_semantics=("parallel", "arbitrary"), vmem_limit_bytes=VMEM_LIMIT),
        name="gdn",
    )(proj_b, proj_b, proj_b, proj_b, proj_b, proj_b, conv_w, a_log_rep, dt_bias_rep, norm_g,
      ctab, gmat, bd)


def _sb_body(q_ref, k_ref, v_ref, ng_ref, later_ref, gmat_ref, o_ref):
    qi = pl.program_id(1)
    blk, unit = SB_BLOCK, SB_UNIT
    hm = _head_masks(GROUP_W, HEAD_W, BF16)
    later = later_ref[...]
    q = q_ref[0] * (HEAD_W ** -0.5)
    qs = jnp.concatenate([q * hm[h] for h in range(N_HEADS)], axis=0)
    t_loc = lax.broadcasted_iota(jnp.int32, (N_HEADS * blk, unit), 0) & (blk - 1)
    s_loc = lax.broadcasted_iota(jnp.int32, (N_HEADS * blk, unit), 1)

    def scores(u, past):
        kblk = k_ref[0, pl.ds(u * unit, unit), :]
        z2 = _dot_nt(qs, kblk) * LOG2_E
        log_sig = jnp.minimum(z2, 0.0) - jnp.log2(1.0 + jnp.exp2(-jnp.abs(z2)))
        log_stay = log_sig - z2
        if past is not None:
            log_stay = jnp.where(past, log_stay, 0.0)
        sfx = _dot(_bf(log_stay), later)
        return log_sig + sfx, jnp.sum(log_stay, axis=-1, keepdims=True)

    def weights(arg, carry, past):
        wts = jnp.exp2(arg + carry)
        if past is not None:
            wts = jnp.where(past, wts, 0.0)
        wts = _bf(wts)
        return jnp.concatenate([wts[h * blk:(h + 1) * blk] for h in range(N_HEADS)], axis=1)

    def values(u):
        vblk = v_ref[0, pl.ds(u * unit, unit), :]
        return jnp.concatenate([vblk * hm[h] for h in range(N_HEADS)], axis=0)

    u0 = (qi * blk) // unit
    u1 = jnp.maximum(u0 - 1, 0)
    past0 = (qi * blk + t_loc) > (u0 * unit + s_loc)
    past1 = jnp.broadcast_to(u0 > 0, past0.shape)
    arg0, total0 = scores(u0, past0)
    arg1, total1 = scores(u1, past1)
    w0 = weights(arg0, 0.0, past0)
    w1 = weights(arg1, total0, past1)
    acc = _dot(jnp.concatenate([w0, w1], axis=1), jnp.concatenate([values(u0), values(u1)], axis=0))
    carry = total0 + total1

    def live(state):
        return jnp.logical_and(state[0] >= 0, state[1])

    def step(state):
        u, _, carry, acc = state
        arg, total = scores(u, None)
        acc = acc + _dot(weights(arg, carry, None), values(u))
        carry = carry + total
        return u - 1, jnp.max(carry) > SB_DEAD, carry, acc

    _, _, carry, acc = lax.while_loop(live, step, (u0 - 2, jnp.max(carry) > SB_DEAD, carry, acc))
    o_ref[0] = _bf(_head_rmsnorm(acc, gmat_ref[...], ng_ref[...]))


def _sb_mixer(proj_c, norm_g, u2, gmat):
    bsz, seq, _ = proj_c.shape
    whole = lambda a: pl.BlockSpec(a.shape, lambda b, t: (0,) * a.ndim)
    return pl.pallas_call(
        _sb_body,
        grid=(bsz, seq // SB_BLOCK),
        in_specs=[pl.BlockSpec((1, SB_BLOCK, GROUP_W), lambda b, t: (b, t, 0)),
                  pl.BlockSpec((1, seq, GROUP_W), lambda b, t: (b, 0, 1)),
                  pl.BlockSpec((1, seq, GROUP_W), lambda b, t: (b, 0, 2)),
                  whole(norm_g), whole(u2), whole(gmat)],
        out_specs=pl.BlockSpec((1, SB_BLOCK, GROUP_W), lambda b, t: (b, t, 0)),
        out_shape=jax.ShapeDtypeStruct((bsz, seq, GROUP_W), BF16),
        compiler_params=pltpu.CompilerParams(
            dimension_semantics=("parallel", "arbitrary"), vmem_limit_bytes=VMEM_LIMIT),
        name="stickbreak",
    )(proj_c, proj_c, proj_c, norm_g, u2, gmat)


def _sb_table():
    j = np.arange(SB_UNIT)[:, None]
    s = np.arange(SB_UNIT)[None, :]
    return jnp.asarray((j > s).astype(np.float32), dtype=BF16)


def _ret_body(q_ref, k_ref, v_ref, g_ref, cos_ref, sin_ref, decay_ref, zeta_ref, xi_ref, gc_ref,
              bd_ref, ng_ref, gmat_ref, o_ref, s_scr):
    @pl.when(pl.program_id(1) == 0)
    def _():
        s_scr[...] = jnp.zeros_like(s_scr)

    chunk = decay_ref.shape[0]
    decay = decay_ref[...]
    lane = lax.broadcasted_iota(jnp.int32, (1, 2 * HEAD_W), 1)
    hq = [(((lane % 64) // 16) == h).astype(BF16) for h in range(N_HEADS)]
    hv = _head_masks(GROUP_W, HEAD_W, BF16)
    gmat = gmat_ref[...]
    ng = ng_ref[...]
    bd = bd_ref[...]
    zeta = zeta_ref[...]
    xi = xi_ref[...]
    gamma_c = gc_ref[...]

    nch = q_ref.shape[1] // chunk
    qx, o_intra, s_add = [], [], []
    for c in range(nch):
        sl = pl.ds(c * chunk, chunk)
        cos = cos_ref[sl, :]
        sin = sin_ref[sl, :]
        q = q_ref[0, sl, :]
        k = k_ref[0, sl, :]
        qr = q * cos + pltpu.roll(q, 64, axis=1) * sin
        kr = (k * cos + pltpu.roll(k, 64, axis=1) * sin) * (D_DK ** -0.5)
        v = _bf(v_ref[0, sl, :])
        sc = _bf(_dot_nt(_bf(qr), _stack_heads(_bf(kr), hq)) * decay)
        o_intra.append(_dot(sc, _stack_heads(v, hv)))
        qx.append(_bf(qr * xi))
        s_add.append(_dot_tn(_bf(kr * zeta), v) * bd)

    s = s_scr[...]
    outs = []
    for c in range(nch):
        outs.append(o_intra[c] + _dot(qx[c], _bf(s)))
        s = s * gamma_c + s_add[c]
    s_scr[...] = s
    o_ref[0] = _bf(_head_rmsnorm(jnp.concatenate(outs, axis=0), gmat, ng) * _silu(g_ref[0]))


def _ret_mixer(proj_d, tables, norm_g, gmat, rows):
    bsz, seq, _ = proj_d.shape
    cos, sin, decay, zeta, xi, gamma_c, bd = tables
    whole = lambda a: pl.BlockSpec(a.shape, lambda b, t: (0,) * a.ndim)
    return pl.pallas_call(
        _ret_body,
        grid=(bsz, seq // rows),
        in_specs=[pl.BlockSpec((1, rows, 128), lambda b, t: (b, t, 0)),
                  pl.BlockSpec((1, rows, 128), lambda b, t: (b, t, 1)),
                  pl.BlockSpec((1, rows, GROUP_W), lambda b, t: (b, t, 1)),
                  pl.BlockSpec((1, rows, GROUP_W), lambda b, t: (b, t, 2)),
                  pl.BlockSpec((rows, 128), lambda b, t: (t, 0)),
                  pl.BlockSpec((rows, 128), lambda b, t: (t, 0)),
                  whole(decay), whole(zeta), whole(xi), whole(gamma_c), whole(bd),
                  whole(norm_g), whole(gmat)],
        out_specs=pl.BlockSpec((1, rows, GROUP_W), lambda b, t: (b, t, 0)),
        out_shape=jax.ShapeDtypeStruct((bsz, seq, GROUP_W), BF16),
        scratch_shapes=[pltpu.VMEM((2 * HEAD_W, GROUP_W), F32)],
        compiler_params=pltpu.CompilerParams(
            dimension_semantics=("parallel", "arbitrary"), vmem_limit_bytes=VMEM_LIMIT),
        name="retention",
    )(proj_d, proj_d, proj_d, proj_d, cos, sin, decay, zeta, xi, gamma_c, bd, norm_g, gmat)


def _mix_weights(w_in):
    w_in = _bf(w_in)
    a_w = w_in[..., 0:1024]
    b0 = 1024
    b_main = w_in[..., b0:b0 + 1024]
    b_a = jnp.repeat(w_in[..., b0 + 1024:b0 + 1028], HEAD_W, axis=-1)
    b_b = jnp.repeat(w_in[..., b0 + 1028:b0 + 1032], HEAD_W, axis=-1)
    c0 = b0 + 1032
    c_w = w_in[..., c0:c0 + 768]
    d0 = c0 + 768
    perm = _ret_perm()
    d_q = w_in[..., d0:d0 + 128][..., perm]
    d_k = w_in[..., d0 + 128:d0 + 256][..., perm]
    d_rest = w_in[..., d0 + 256:d0 + 768]
    b_w = jnp.concatenate([b_main, b_a, b_b], axis=-1)
    d_w = jnp.concatenate([d_q, d_k, d_rest], axis=-1)
    return a_w, b_w, c_w, d_w


def kernel(x, ffn1_norm, ffn1_w_in, ffn1_w_out, mix_norm, mix_w_in, mix_w_out, ffn2_norm, ffn2_w_in,
           ffn2_w_out, hgrn_lb_logits, hgrn_out_norm, gdn_conv_w, gdn_a_log, gdn_dt_bias, gdn_out_norm,
           sb_out_norm, ret_out_norm, final_norm):
    bsz, seq, d = x.shape
    depth = ffn1_norm.shape[0]
    n = bsz * seq
    d_ff = ffn1_w_out.shape[1]
    tm_ffn = 1024
    tm = 512
    rows = MIXER_ROWS

    ctab = _cumsum_table()
    gmat = _head_block_matrix(GROUP_W, HEAD_W, GROUP_W, HEAD_W, BF16)
    bd = _head_block_matrix(GROUP_W, HEAD_W, GROUP_W, HEAD_W, F32)
    u2 = _sb_table()
    ret_tables = _retention_tables(seq, RET_CHUNK)
    row = lambda a: a.reshape(1, -1).astype(F32)
    rep = lambda a: jnp.repeat(a.astype(F32), HEAD_W).reshape(1, GROUP_W)

    w1_in, w1_out, w2_in, w2_out, w_mix_out = (_bf(w) for w in (ffn1_w_in, ffn1_w_out, ffn2_w_in, ffn2_w_out,
                                                                  mix_w_out))
    w_mix_in = _mix_weights(mix_w_in)

    x = x.reshape(n, d)
    for l in range(depth):
        x = _ffn(x, l, row(ffn1_norm[l]), w1_in, w1_out, tm=tm_ffn)
        pa, pb, pc, pd = _mix_proj(x, l, row(mix_norm[l]), w_mix_in, tm=tm)
        shp = lambda a: a.reshape(bsz, seq, a.shape[-1])
        ya = _hgrn_mixer(shp(pa), hgrn_lb_logits.astype(F32), row(hgrn_out_norm[l]), ctab, gmat, bd, l, rows)
        yb = _gdn_mixer(shp(pb), gdn_conv_w[l].astype(F32), rep(gdn_a_log[l]), rep(gdn_dt_bias[l]),
                        row(gdn_out_norm[l]), ctab, gmat, bd, GDN_ROWS)
        yc = _sb_mixer(shp(pc), row(sb_out_norm[l]), u2, gmat)
        yd = _ret_mixer(shp(pd), ret_tables, row(ret_out_norm[l]), gmat, rows)
        ys = [y.reshape(n, GROUP_W) for y in (ya, yb, yc, yd)]
        x = _ffn(x, l, row(ffn2_norm[l]), w2_in, w2_out, tm=tm_ffn, mix=(ys, w_mix_out),
                 final_gain=row(final_norm) if l == depth - 1 else None)
    return x.reshape(bsz, seq, d)
```

```python
import functools
import math

import numpy as np
import jax
import jax.numpy as jnp
from jax import lax
from jax.experimental import pallas as pl
from jax.experimental.pallas import tpu as pltpu

F32 = jnp.float32
BF16 = jnp.bfloat16

EPS = 1e-6
CHUNK = 64
N_HEADS = 4
HEAD_W = 64
GROUP_W = N_HEADS * HEAD_W
D_DK = 32
ROPE_BASE = 10000.0
N_LEVELS = 6
VMEM_LIMIT = 56 * 1024 * 1024

MIXER_ROWS = 512
GDN_GROUP = 4
FFN_COLS = 256
RET_CHUNK = 128
SB_BLOCK = 256
SB_UNIT = 256
SB_DEAD = -151.0
LOG2_E = 1.4426950408889634


def _bf(x):
    return x.astype(BF16)


def _dot(a, b):
    return jnp.dot(a, b, preferred_element_type=F32)


def _dot_nt(a, b):
    return lax.dot_general(a, b, (((1,), (1,)), ((), ())), preferred_element_type=F32)


def _dot_tn(a, b):
    return lax.dot_general(a, b, (((0,), (0,)), ((), ())), preferred_element_type=F32)


def _split2(x):
    hi = _bf(x)
    lo = _bf(x - hi.astype(F32))
    return hi, lo


def _split3(x):
    h1 = _bf(x)
    r = x - h1.astype(F32)
    h2 = _bf(r)
    h3 = _bf(r - h2.astype(F32))
    return h1, h2, h3


def _select_rows(w3, x):
    return _dot(w3, jnp.concatenate(_split3(x), axis=0))


def _headsum(x, gmat):
    hi, lo = _split2(x)
    return _dot(hi, gmat) + _dot(lo, gmat)


def _head_rmsnorm(o, gmat, gain):
    ms = _headsum(o * o, gmat) * (1.0 / HEAD_W)
    return o * lax.rsqrt(ms + EPS) * gain


def _sigmoid(x):
    return 0.5 * jnp.tanh(0.5 * x) + 0.5


def _sigmoid_rel(x):
    return jnp.exp(jnp.minimum(x, 0.0) - jnp.log(1.0 + jnp.exp(-jnp.abs(x))))


def _silu(x):
    return x * _sigmoid(x)


def _softplus(x):
    return jnp.maximum(x, 0.0) + jnp.log(1.0 + jnp.exp(-jnp.abs(x)))


def _head_masks(width, lanes_per_head, dtype):
    lane = lax.broadcasted_iota(jnp.int32, (1, width), 1)
    return [((lane // lanes_per_head) == h).astype(dtype) for h in range(N_HEADS)]


def _stack_heads(x, masks):
    return jnp.concatenate([x * m for m in masks], axis=0)


def _level_masks(lane_stack=1):
    r = lax.broadcasted_iota(jnp.int32, (CHUNK, lane_stack * CHUNK), 0)
    c = lax.broadcasted_iota(jnp.int32, (CHUNK, lane_stack * CHUNK), 1) & (CHUNK - 1)
    pair = []
    for l in range(N_LEVELS):
        n = CHUNK >> l
        m = n // 2
        sh = N_LEVELS - l
        same = (r >> sh) == (c >> sh)
        pair.append((same & ((r & (n - 1)) >= m) & ((c & (n - 1)) < m)).astype(F32))
    eye = (r == c).astype(F32)
    return pair, eye


def _decay_factors(cs):
    return [(jnp.exp(cs[(2 + l) * CHUNK:(3 + l) * CHUNK]),
             jnp.exp(cs[(2 + N_LEVELS + l) * CHUNK:(3 + N_LEVELS + l) * CHUNK])) for l in range(N_LEVELS)]


def _cumsum_table():
    t = np.arange(CHUNK)
    tri = (t[None, :] <= t[:, None]).astype(np.float32)
    up = (t[None, :] > t[:, None]).astype(np.float32)
    q_blocks, k_blocks = [], []
    for l in range(N_LEVELS):
        n = CHUNK >> l
        m = n // 2
        anchor = (t // n) * n + m - 1
        upper = ((t % n) >= m)[:, None]
        q_blocks.append(np.where(upper, tri - tri[anchor], 0.0))
        k_blocks.append(np.where(~upper, tri[anchor] - tri, 0.0))
    tab = np.concatenate([tri, up] + q_blocks + k_blocks, axis=0)
    assert set(np.unique(tab)) <= {0.0, 1.0}
    return jnp.asarray(np.concatenate([tab, tab, tab], axis=1), dtype=BF16)


def _head_block_matrix(rows, rows_per_head, cols, cols_per_head, dtype):
    r = np.arange(rows)[:, None] // rows_per_head
    c = np.arange(cols)[None, :] // cols_per_head
    return jnp.asarray((r == c).astype(np.float32), dtype=dtype)


def _ret_lane_head(p):
    return (p % 64) // 16


def _retention_tables(seq, chunk):
    p = np.arange(128)
    half = D_DK // 2
    inv_freq = ROPE_BASE ** (-(p % 16).astype(np.float64) / half)
    ang = np.arange(seq, dtype=np.float64)[:, None] * inv_freq[None, :]
    cos = np.cos(ang)
    sin = np.sin(ang) * np.where(p < 64, -1.0, 1.0)[None, :]
    log_gamma = np.log(1.0 - 2.0 ** (-5.0 - np.arange(N_HEADS, dtype=np.float64)))
    c = np.arange(chunk, dtype=np.float64)
    rel = c[:, None] - c[None, :]
    decay = np.where(rel[None] >= 0, np.exp(rel[None] * log_gamma[:, None, None]), 0.0)
    decay = decay.transpose(1, 0, 2).reshape(chunk, N_HEADS * chunk)
    lane_h = _ret_lane_head(p)
    zeta = np.exp((chunk - 1 - c)[:, None] * log_gamma[lane_h][None, :])
    xi = np.exp((c + 1.0)[:, None] * log_gamma[lane_h][None, :])
    gamma_c = np.exp(chunk * log_gamma)[np.arange(GROUP_W) // HEAD_W][None, :]
    bd = (lane_h[:, None] == (np.arange(GROUP_W) // HEAD_W)[None, :]).astype(np.float32)
    f = lambda a: jnp.asarray(a, dtype=F32)
    return f(cos), f(sin), f(decay), f(zeta), f(xi), f(gamma_c), f(bd)


def _ret_perm():
    p = np.arange(128)
    h = _ret_lane_head(p)
    return h * D_DK + (p % 16) + np.where(p >= 64, 16, 0)


def _rms_rows(x, gain):
    return x * lax.rsqrt(jnp.mean(x * x, axis=-1, keepdims=True) + EPS) * gain


def _ffn_body(has_mix, has_final, *refs):
    refs = list(refs)
    x_ref = refs.pop(0)
    y_refs = [refs.pop(0) for _ in range(4)] if has_mix else []
    wmix_ref = refs.pop(0) if has_mix else None
    g_ref, win_ref, wout_ref = refs.pop(0), refs.pop(0), refs.pop(0)
    gf_ref = refs.pop(0) if has_final else None
    o_ref = refs.pop(0)

    x = x_ref[...]
    for m, y_ref in enumerate(y_refs):
        x = x + _dot(y_ref[...], wmix_ref[m * GROUP_W:(m + 1) * GROUP_W, :])
    h = _bf(_rms_rows(x, g_ref[...]))

    d_ff = wout_ref.shape[0]
    steps = d_ff // FFN_COLS
    acc = None
    act = None
    for c in range(steps + 1):
        if c < steps:
            gate = _dot(h, win_ref[:, c * FFN_COLS:(c + 1) * FFN_COLS])
            up = _dot(h, win_ref[:, d_ff + c * FFN_COLS:d_ff + (c + 1) * FFN_COLS])
        if act is not None:
            down = _dot(act, wout_ref[(c - 1) * FFN_COLS:c * FFN_COLS, :])
            acc = down if acc is None else acc + down
        if c < steps:
            act = _bf(_silu(gate) * up)
    y = x + 0.5 * acc
    if has_final:
        y = _rms_rows(y, gf_ref[...])
    o_ref[...] = y


def _layer_spec(a, layer):
    return pl.BlockSpec((None,) + a.shape[1:], lambda i: (layer, 0, 0), pipeline_mode=pl.Buffered(1))


def _ffn(x, layer, gain, w_in, w_out, *, tm, mix=None, final_gain=None):
    n, d = x.shape
    rows = lambda w: pl.BlockSpec((tm, w), lambda i: (i, 0))
    whole = lambda a: _layer_spec(a, layer) if a.ndim == 3 else pl.BlockSpec(a.shape, lambda i: (0, 0))
    args, specs = [x], [rows(d)]
    if mix is not None:
        ys, w_mix = mix
        args += list(ys) + [w_mix]
        specs += [rows(GROUP_W) for _ in ys] + [whole(w_mix)]
    args += [gain, w_in, w_out]
    specs += [whole(gain), whole(w_in), whole(w_out)]
    if final_gain is not None:
        args.append(final_gain)
        specs.append(whole(final_gain))
    return pl.pallas_call(
        functools.partial(_ffn_body, mix is not None, final_gain is not None),
        grid=(n // tm,),
        in_specs=specs,
        out_specs=rows(d),
        out_shape=jax.ShapeDtypeStruct((n, d), F32),
        compiler_params=pltpu.CompilerParams(
            dimension_semantics=("parallel",), vmem_limit_bytes=VMEM_LIMIT),
        name="ffn",
    )(*args)


def _proj_body(x_ref, g_ref, wa_ref, wb_ref, wc_ref, wd_ref, oa_ref, ob_ref, oc_ref, od_ref):
    h = _bf(_rms_rows(x_ref[...], g_ref[...]))
    oa_ref[...] = _dot(h, wa_ref[...])
    ob_ref[...] = _dot(h, wb_ref[...])
    oc_ref[...] = _bf(_dot(h, wc_ref[...]))
    od_ref[...] = _dot(h, wd_ref[...])


def _mix_proj(x, layer, gain, ws, *, tm):
    n, d = x.shape
    dts = (F32, F32, BF16, F32)
    return pl.pallas_call(
        _proj_body,
        grid=(n // tm,),
        in_specs=[pl.BlockSpec((tm, d), lambda i: (i, 0)), pl.BlockSpec((1, d), lambda i: (0, 0))]
        + [_layer_spec(w, layer) for w in ws],
        out_specs=[pl.BlockSpec((tm, w.shape[2]), lambda i: (i, 0)) for w in ws],
        out_shape=[jax.ShapeDtypeStruct((n, w.shape[2]), dt) for w, dt in zip(ws, dts)],
        compiler_params=pltpu.CompilerParams(
            dimension_semantics=("parallel",), vmem_limit_bytes=VMEM_LIMIT),
        name="mix_proj",
    )(x, gain, *ws)


def _hgrn_steps(layer, q_ref, f_ref, i_ref, g_ref, lbl_ref, ng_ref, ctab_ref, gmat_ref, bd_ref,
                o_ref, st_scr):
    @pl.when(pl.program_id(1) == 0)
    def _():
        st_scr[...] = jnp.zeros_like(st_scr)

    logits = lbl_ref[...]
    e = jnp.exp(logits - jnp.max(logits, axis=0, keepdims=True))
    p = e / jnp.sum(e, axis=0, keepdims=True)
    lb = jnp.zeros_like(p[0:1])
    for r in range(1, layer + 1):
        lb = lb + p[r:r + 1]

    pair, eye = _level_masks(lane_stack=N_HEADS)
    hm = _head_masks(GROUP_W, HEAD_W, BF16)
    ctab = ctab_ref[...]
    gmat = gmat_ref[...]
    bd = bd_ref[...]
    ng = ng_ref[...]

    chunks = range(q_ref.shape[1] // CHUNK)
    rows_of = lambda c: pl.ds(c * CHUNK, CHUNK)
    qs, kk, cs = [], [], []
    for c in chunks:
        f = lb + (1.0 - lb) * _sigmoid_rel(f_ref[0, rows_of(c), :])
        qs.append(_silu(q_ref[0, rows_of(c), :]))
        kk.append(1.0 - f)
        cs.append(_select_rows(ctab, jnp.log(f)))
        yield 250

    scores = []
    for c in chunks:
        sc = eye * _dot_nt(_bf(qs[c]), _stack_heads(_bf(kk[c]), hm))
        for l, (fq, fk) in enumerate(_decay_factors(cs[c])):
            sc = sc + pair[l] * _dot_nt(_bf(qs[c] * fq), _stack_heads(_bf(kk[c] * fk), hm))
        scores.append(_bf(sc))
        yield 450

    qdec, b_last, o_intra, st_add = [], [], [], []
    for c in chunks:
        b = cs[c][0:CHUNK]
        rem = cs[c][CHUNK:2 * CHUNK]
        v = _bf(_silu(i_ref[0, rows_of(c), :]))
        o_intra.append(_dot(scores[c], _stack_heads(v, hm)))
        qdec.append(_bf(qs[c] * jnp.exp(b)))
        st_add.append(_dot_tn(v, _bf(kk[c] * jnp.exp(rem))) * bd)
        b_last.append(jnp.exp(b[CHUNK - 1:CHUNK]))
        yield 200

    st = st_scr[...]
    outs = []
    for c in chunks:
        outs.append(o_intra[c] + _dot_nt(qdec[c], _bf(st)))
        st = st * b_last[c] + st_add[c]
        yield 100
    st_scr[...] = st
    o_ref[0] = _bf(_head_rmsnorm(jnp.concatenate(outs, axis=0), gmat, ng) * _sigmoid(g_ref[0]))
    yield 150


def _weave(gens):
    spent = [0.0] * len(gens)
    live = list(range(len(gens)))
    while live:
        i = min(live, key=lambda j: spent[j])
        try:
            spent[i] += next(gens[i])
        except StopIteration:
            live.remove(i)


def _whole_spec(a):
    return pl.BlockSpec(a.shape, lambda b, t: (0,) * a.ndim)


def _col_spec(rows, width, c):
    return pl.BlockSpec((1, rows, width), lambda b, t: (b, t, c))


def _recurrent_mixers(mixers, bsz, seq, rows, name):
    n_in = [len(m[1]) for m in mixers]
    n_scr = [len(m[3]) for m in mixers]

    def body(*refs):
        ins, outs, scr = refs[:sum(n_in)], refs[sum(n_in):sum(n_in) + len(mixers)], refs[sum(n_in) + len(mixers):]
        gens = []
        for i, m in enumerate(mixers):
            a, s = sum(n_in[:i]), sum(n_scr[:i])
            gens.append(m[0](*ins[a:a + n_in[i]], outs[i], *scr[s:s + n_scr[i]]))
        _weave(gens)

    out = pl.pallas_call(
        body,
        grid=(bsz, seq // rows),
        in_specs=[s for m in mixers for s in m[2]],
        out_specs=[pl.BlockSpec((1, rows, GROUP_W), lambda b, t: (b, t, 0)) for _ in mixers],
        out_shape=[jax.ShapeDtypeStruct((bsz, seq, GROUP_W), BF16) for _ in mixers],
        scratch_shapes=[s for m in mixers for s in m[3]],
        compiler_params=pltpu.CompilerParams(
            dimension_semantics=("parallel", "arbitrary"), vmem_limit_bytes=VMEM_LIMIT),
        name=name,
    )(*[a for m in mixers for a in m[1]])
    return out


def _hgrn_io(proj_a, lb_logits, norm_g, ctab, gmat, bd, layer, rows):
    small = [lb_logits, norm_g, ctab, gmat, bd]
    return (functools.partial(_hgrn_steps, layer), [proj_a] * 4 + small,
            [_col_spec(rows, GROUP_W, c) for c in range(4)] + [_whole_spec(a) for a in small],
            [pltpu.VMEM((GROUP_W, GROUP_W), F32)])


def _hgrn_mixer(proj_a, lb_logits, norm_g, ctab, gmat, bd, layer, rows):
    bsz, seq, _ = proj_a.shape
    return _recurrent_mixers([_hgrn_io(proj_a, lb_logits, norm_g, ctab, gmat, bd, layer, rows)],
                             bsz, seq, rows, "hgrn2")[0]


def _gdn_steps(q_ref, k_ref, v_ref, g_ref, a_ref, bl_ref, cw_ref, alog_ref, dtb_ref, ng_ref,
               ctab_ref, gmat_ref, bd_ref, o_ref, s_scr, tail_scr, ext_scr):
    @pl.when(pl.program_id(1) == 0)
    def _():
        s_scr[...] = jnp.zeros_like(s_scr)
        tail_scr[...] = jnp.zeros_like(tail_scr)

    pair, eye = _level_masks()
    hm = _head_masks(GROUP_W, HEAD_W, BF16)
    hmf = _head_masks(GROUP_W, HEAD_W, F32)
    ctab = ctab_ref[...]
    gmat = gmat_ref[...]
    bd = bd_ref[...]
    ng = ng_ref[...]
    neg_a = -jnp.exp(alog_ref[...])
    dtb = dtb_ref[...]
    raw = (q_ref, k_ref, v_ref)
    rows = q_ref.shape[1]

    for j in range(3):
        ext_scr[0:8, j * GROUP_W:(j + 1) * GROUP_W] = tail_scr[:, j * GROUP_W:(j + 1) * GROUP_W]
        ext_scr[8:8 + CHUNK, j * GROUP_W:(j + 1) * GROUP_W] = raw[j][0, 0:CHUNK, :]
        tail_scr[:, j * GROUP_W:(j + 1) * GROUP_W] = raw[j][0, rows - 8:rows, :]

    def conv(c, j):
        w = cw_ref[:, j * GROUP_W:(j + 1) * GROUP_W]
        acc = None
        for d in range(4):
            if c == 0:
                x = ext_scr[pl.ds(8 - d, CHUNK), j * GROUP_W:(j + 1) * GROUP_W]
            else:
                x = raw[j][0, pl.ds(c * CHUNK - d, CHUNK), :]
            term = x * w[3 - d:4 - d, :]
            acc = term if acc is None else acc + term
        return _silu(acc)

    nch = rows // CHUNK
    rr = lax.broadcasted_iota(jnp.int32, (N_HEADS * CHUNK, CHUNK), 0) & (CHUNK - 1)
    cc = lax.broadcasted_iota(jnp.int32, (N_HEADS * CHUNK, CHUNK), 1)
    tril = rr >= cc
    tril_f = tril.astype(F32)
    stril_f = (rr > cc).astype(F32)
    sub = lax.broadcasted_iota(jnp.int32, (1, GROUP_W), 1) & (HEAD_W - 1)
    lane_is = [(sub == i).astype(BF16) for i in range(6)]
    qc, kc, vc, qn, kn, cs, diff, prod, kb, beta = ({} for _ in range(10))
    qg, kdec, vb, kbg, g_last, attn, m_mat, t_inv = ({} for _ in range(8))
    o_lhs, o_add, s_lhs, s_add = {}, {}, {}, {}

    def phase1(grp):
        for c in grp:
            qc[c] = conv(c, 0)
            kc[c] = conv(c, 1)
            vc[c] = conv(c, 2)
            qn[c] = _headsum(qc[c] * qc[c], gmat)
            kn[c] = _headsum(kc[c] * kc[c], gmat)
            log_alpha = neg_a * _softplus(a_ref[0, pl.ds(c * CHUNK, CHUNK), :] + dtb)
            cs[c] = _select_rows(ctab[0:2 * CHUNK], log_alpha)
            yield
        for c in grp:
            qc[c] = qc[c] * lax.rsqrt(qn[c] + EPS) * (HEAD_W ** -0.5)
            kc[c] = kc[c] * lax.rsqrt(kn[c] + EPS)
            beta[c] = _sigmoid(bl_ref[0, pl.ds(c * CHUNK, CHUNK), :])
            kb[c] = kc[c] * beta[c]
            g1, g2, g3 = _split3(cs[c][0:CHUNK])
            lhs = g1 * lane_is[0] + g2 * lane_is[1] + g3 * lane_is[2] + (lane_is[3] + lane_is[4] + lane_is[5])
            rhs = (lane_is[0] + lane_is[1] + lane_is[2]) - g1 * lane_is[3] - g2 * lane_is[4] - g3 * lane_is[5]
            diff[c] = _dot_nt(_stack_heads(lhs, hm), rhs)
            prod[c] = _dot_nt(jnp.concatenate([_stack_heads(_bf(qc[c]), hm), _stack_heads(_bf(kb[c]), hm)],
                                              axis=0), _bf(kc[c]))
            yield
        for c in grp:
            gc = cs[c][0:CHUNK]
            rem = cs[c][CHUNK:2 * CHUNK]
            dec = jnp.exp(jnp.where(tril, diff[c], 0.0))
            attn_st = _bf(tril_f * dec * prod[c][0:N_HEADS * CHUNK])
            m_st = stril_f * dec * prod[c][N_HEADS * CHUNK:2 * N_HEADS * CHUNK]
            for h in range(N_HEADS):
                attn[c, h] = attn_st[h * CHUNK:(h + 1) * CHUNK]
                m_mat[c, h] = m_st[h * CHUNK:(h + 1) * CHUNK]
            qg[c] = qc[c] * jnp.exp(gc)
            kdec[c] = _bf(kc[c] * jnp.exp(rem))
            vb[c] = vc[c] * beta[c]
            kbg[c] = kb[c] * jnp.exp(gc)
            g_last[c] = jnp.exp(gc[CHUNK - 1:CHUNK])
            yield

    def phase2(grp):
        pairs = [(c, h) for c in grp for h in range(N_HEADS)]
        for p in pairs:
            t_inv[p] = eye - pair[N_LEVELS - 1] * m_mat[p]
        for l in range(N_LEVELS - 2, -1, -1):
            t_b = {p: _bf(t_inv[p]) for p in pairs}
            tl = {p: _dot(t_b[p], _bf(pair[l] * m_mat[p])) for p in pairs}
            yield
            for p in pairs:
                t_inv[p] = t_inv[p] - _dot(_bf(tl[p]), t_b[p])
            yield

    def phase3(grp):
        uw = {}
        for c in grp:
            rhs = jnp.concatenate([vb[c], kbg[c]], axis=1)
            acc = None
            for h in range(N_HEADS):
                lhs = jnp.concatenate(_split2(t_inv[c, h]), axis=1)
                rhs_h = _bf(rhs * jnp.concatenate([hmf[h], hmf[h]], axis=1))
                term = _dot(lhs, jnp.concatenate([rhs_h, rhs_h], axis=0))
                acc = term if acc is None else acc + term
            uw[c] = _bf(acc)
            yield
        for c in grp:
            u_b = uw[c][:, 0:GROUP_W]
            w_b = uw[c][:, GROUP_W:2 * GROUP_W]
            aw = None
            for h in range(N_HEADS):
                term = _dot(attn[c, h], jnp.concatenate([u_b * hm[h], w_b * hm[h]], axis=1))
                aw = term if aw is None else aw + term
            o_add[c] = aw[:, 0:GROUP_W]
            o_lhs[c] = _bf(qg[c] - aw[:, GROUP_W:2 * GROUP_W])
            ks = _dot_tn(kdec[c], uw[c])
            s_add[c] = ks[:, 0:GROUP_W] * bd
            s_lhs[c] = _bf(-ks[:, GROUP_W:2 * GROUP_W] * bd)
            yield

    state = [s_scr[...]]
    outs = []

    def phase4(grp):
        for c in grp:
            s_b = _bf(state[0])
            outs.append(o_add[c] + _dot(o_lhs[c], s_b))
            state[0] = state[0] * g_last[c] + _dot(s_lhs[c], s_b) + s_add[c]
            yield

    def chain(*gens):
        for gen in gens:
            yield from gen

    def run(primary, secondary=iter(()), per_step=1):
        for _ in primary:
            for _ in range(per_step):
                next(secondary, None)
            yield 400
        for _ in secondary:
            yield 200

    groups = [range(g, min(g + GDN_GROUP, nch)) for g in range(0, nch, GDN_GROUP)]
    yield from run(phase1(groups[0]))
    for gi, grp in enumerate(groups):
        fill = []
        if gi > 0:
            fill += [phase3(groups[gi - 1]), phase4(groups[gi - 1])]
        if gi + 1 < len(groups):
            fill.append(phase1(groups[gi + 1]))
        yield from run(phase2(grp), chain(*fill), per_step=2)
    yield from run(chain(phase3(groups[-1]), phase4(groups[-1])))
    s_scr[...] = state[0]
    o_ref[0] = _bf(_head_rmsnorm(jnp.concatenate(outs, axis=0), gmat, ng) * _silu(g_ref[0]))
    yield 200


def _gdn_io(proj_b, conv_w, a_log_rep, dt_bias_rep, norm_g, ctab, gmat, bd, rows):
    small = [conv_w, a_log_rep, dt_bias_rep, norm_g, ctab, gmat, bd]
    return (_gdn_steps, [proj_b] * 6 + small,
            [_col_spec(rows, GROUP_W, c) for c in range(6)] + [_whole_spec(a) for a in small],
            [pltpu.VMEM((GROUP_W, GROUP_W), F32),
             pltpu.VMEM((8, 3 * GROUP_W), F32),
             pltpu.VMEM((8 + CHUNK, 3 * GROUP_W), F32)])


def _gdn_mixer(proj_b, conv_w, a_log_rep, dt_bias_rep, norm_g, ctab, gmat, bd, rows):
    bsz, seq, _ = proj_b.shape
    return _recurrent_mixers([_gdn_io(proj_b, conv_w, a_log_rep, dt_bias_rep, norm_g, ctab, gmat, bd, rows)],
                             bsz, seq, rows, "gdn")[0]


def _sb_body(q_ref, k_ref, v_ref, ng_ref, later_ref, gmat_ref, o_ref):
    qi = pl.program_id(1)
    blk, unit = SB_BLOCK, SB_UNIT
    hm = _head_masks(GROUP_W, HEAD_W, BF16)
    later = later_ref[...]
    q = q_ref[0] * (HEAD_W ** -0.5)
    qs = jnp.concatenate([q * hm[h] for h in range(N_HEADS)], axis=0)
    t_loc = lax.broadcasted_iota(jnp.int32, (N_HEADS * blk, unit), 0) & (blk - 1)
    s_loc = lax.broadcasted_iota(jnp.int32, (N_HEADS * blk, unit), 1)

    def scores(u, past):
        kblk = k_ref[0, pl.ds(u * unit, unit), :]
        z2 = _dot_nt(qs, kblk) * LOG2_E
        log_sig = jnp.minimum(z2, 0.0) - jnp.log2(1.0 + jnp.exp2(-jnp.abs(z2)))
        log_stay = log_sig - z2
        if past is not None:
            log_stay = jnp.where(past, log_stay, 0.0)
        sfx = _dot(_bf(log_stay), later)
        return log_sig + sfx, jnp.sum(log_stay, axis=-1, keepdims=True)

    def weights(arg, carry, past):
        wts = jnp.exp2(arg + carry)
        if past is not None:
            wts = jnp.where(past, wts, 0.0)
        wts = _bf(wts)
        return jnp.concatenate([wts[h * blk:(h + 1) * blk] for h in range(N_HEADS)], axis=1)

    def values(u, scale=None):
        vblk = v_ref[0, pl.ds(u * unit, unit), :]
        if scale is not None:
            vblk = vblk * scale
        return jnp.concatenate([vblk * hm[h] for h in range(N_HEADS)], axis=0)

    assert blk == unit
    u0 = qi
    has_prev = (qi > 0).astype(F32)
    past0 = t_loc > s_loc
    arg0, total0 = scores(u0, past0)
    arg1, total1 = scores(jnp.maximum(u0 - 1, 0), None)
    w0 = weights(arg0, 0.0, past0)
    w1 = weights(arg1, total0, None)
    v1 = values(jnp.maximum(u0 - 1, 0), jnp.broadcast_to(has_prev, (1, GROUP_W)).astype(BF16))
    acc = _dot(jnp.concatenate([w0, w1], axis=1), jnp.concatenate([values(u0), v1], axis=0))
    carry = total0 + total1 * has_prev

    def live(state):
        return jnp.logical_and(state[0] >= 0, state[1])

    def step(state):
        u, _, carry, acc = state
        arg, total = scores(u, None)
        acc = acc + _dot(weights(arg, carry, None), values(u))
        carry = carry + total
        return u - 1, jnp.max(carry) > SB_DEAD, carry, acc

    _, _, carry, acc = lax.while_loop(live, step, (u0 - 2, jnp.max(carry) > SB_DEAD, carry, acc))
    o_ref[0] = _bf(_head_rmsnorm(acc, gmat_ref[...], ng_ref[...]))


def _sb_mixer(proj_c, norm_g, u2, gmat):
    bsz, seq, _ = proj_c.shape
    whole = lambda a: pl.BlockSpec(a.shape, lambda b, t: (0,) * a.ndim)
    return pl.pallas_call(
        _sb_body,
        grid=(bsz, seq // SB_BLOCK),
        in_specs=[pl.BlockSpec((1, SB_BLOCK, GROUP_W), lambda b, t: (b, t, 0)),
                  pl.BlockSpec((1, seq, GROUP_W), lambda b, t: (b, 0, 1)),
                  pl.BlockSpec((1, seq, GROUP_W), lambda b, t: (b, 0, 2)),
                  whole(norm_g), whole(u2), whole(gmat)],
        out_specs=pl.BlockSpec((1, SB_BLOCK, GROUP_W), lambda b, t: (b, t, 0)),
        out_shape=jax.ShapeDtypeStruct((bsz, seq, GROUP_W), BF16),
        compiler_params=pltpu.CompilerParams(
            dimension_semantics=("parallel", "arbitrary"), vmem_limit_bytes=VMEM_LIMIT),
        name="stickbreak",
    )(proj_c, proj_c, proj_c, norm_g, u2, gmat)


def _sb_table():
    j = np.arange(SB_UNIT)[:, None]
    s = np.arange(SB_UNIT)[None, :]
    return jnp.asarray((j > s).astype(np.float32), dtype=BF16)


def _ret_steps(q_ref, k_ref, v_ref, g_ref, cos_ref, sin_ref, decay_ref, zeta_ref, xi_ref, gc_ref,
               bd_ref, ng_ref, gmat_ref, o_ref, s_scr):
    @pl.when(pl.program_id(1) == 0)
    def _():
        s_scr[...] = jnp.zeros_like(s_scr)

    chunk = decay_ref.shape[0]
    decay = decay_ref[...]
    lane = lax.broadcasted_iota(jnp.int32, (1, 2 * HEAD_W), 1)
    hq = [(((lane % 64) // 16) == h).astype(BF16) for h in range(N_HEADS)]
    hv = _head_masks(GROUP_W, HEAD_W, BF16)
    gmat = gmat_ref[...]
    ng = ng_ref[...]
    bd = bd_ref[...]
    zeta = zeta_ref[...]
    xi = xi_ref[...]
    gamma_c = gc_ref[...]

    nch = q_ref.shape[1] // chunk
    qx, o_intra, s_add = [], [], []
    for c in range(nch):
        sl = pl.ds(c * chunk, chunk)
        cos = cos_ref[sl, :]
        sin = sin_ref[sl, :]
        q = q_ref[0, sl, :]
        k = k_ref[0, sl, :]
        qr = q * cos + pltpu.roll(q, 64, axis=1) * sin
        kr = (k * cos + pltpu.roll(k, 64, axis=1) * sin) * (D_DK ** -0.5)
        v = _bf(v_ref[0, sl, :])
        sc = _bf(_dot_nt(_bf(qr), _stack_heads(_bf(kr), hq)) * decay)
        o_intra.append(_dot(sc, _stack_heads(v, hv)))
        qx.append(_bf(qr * xi))
        s_add.append(_dot_tn(_bf(kr * zeta), v) * bd)
        yield 350

    s = s_scr[...]
    outs = []
    for c in range(nch):
        outs.append(o_intra[c] + _dot(qx[c], _bf(s)))
        s = s * gamma_c + s_add[c]
        yield 80
    s_scr[...] = s
    o_ref[0] = _bf(_head_rmsnorm(jnp.concatenate(outs, axis=0), gmat, ng) * _silu(g_ref[0]))
    yield 150


def _ret_io(proj_d, tables, norm_g, gmat, rows):
    cos, sin, decay, zeta, xi, gamma_c, bd = tables
    small = [decay, zeta, xi, gamma_c, bd, norm_g, gmat]
    return (_ret_steps, [proj_d] * 4 + [cos, sin] + small,
            [_col_spec(rows, 128, 0), _col_spec(rows, 128, 1), _col_spec(rows, GROUP_W, 1),
             _col_spec(rows, GROUP_W, 2),
             pl.BlockSpec((rows, 128), lambda b, t: (t, 0)), pl.BlockSpec((rows, 128), lambda b, t: (t, 0))]
            + [_whole_spec(a) for a in small],
            [pltpu.VMEM((2 * HEAD_W, GROUP_W), F32)])


def _ret_mixer(proj_d, tables, norm_g, gmat, rows):
    bsz, seq, _ = proj_d.shape
    return _recurrent_mixers([_ret_io(proj_d, tables, norm_g, gmat, rows)], bsz, seq, rows, "retention")[0]


def _mix_weights(w_in):
    w_in = _bf(w_in)
    a_w = w_in[..., 0:1024]
    b0 = 1024
    b_main = w_in[..., b0:b0 + 1024]
    b_a = jnp.repeat(w_in[..., b0 + 1024:b0 + 1028], HEAD_W, axis=-1)
    b_b = jnp.repeat(w_in[..., b0 + 1028:b0 + 1032], HEAD_W, axis=-1)
    c0 = b0 + 1032
    c_w = w_in[..., c0:c0 + 768]
    d0 = c0 + 768
    perm = _ret_perm()
    d_q = w_in[..., d0:d0 + 128][..., perm]
    d_k = w_in[..., d0 + 128:d0 + 256][..., perm]
    d_rest = w_in[..., d0 + 256:d0 + 768]
    b_w = jnp.concatenate([b_main, b_a, b_b], axis=-1)
    d_w = jnp.concatenate([d_q, d_k, d_rest], axis=-1)
    return a_w, b_w, c_w, d_w


def kernel(x, ffn1_norm, ffn1_w_in, ffn1_w_out, mix_norm, mix_w_in, mix_w_out, ffn2_norm, ffn2_w_in,
           ffn2_w_out, hgrn_lb_logits, hgrn_out_norm, gdn_conv_w, gdn_a_log, gdn_dt_bias, gdn_out_norm,
           sb_out_norm, ret_out_norm, final_norm):
    bsz, seq, d = x.shape
    depth = ffn1_norm.shape[0]
    n = bsz * seq
    d_ff = ffn1_w_out.shape[1]
    tm_ffn = 1024
    tm = 512
    rows = MIXER_ROWS

    ctab = _cumsum_table()
    gmat = _head_block_matrix(GROUP_W, HEAD_W, GROUP_W, HEAD_W, BF16)
    bd = _head_block_matrix(GROUP_W, HEAD_W, GROUP_W, HEAD_W, F32)
    u2 = _sb_table()
    ret_tables = _retention_tables(seq, RET_CHUNK)
    row = lambda a: a.reshape(1, -1).astype(F32)
    rep = lambda a: jnp.repeat(a.astype(F32), HEAD_W).reshape(1, GROUP_W)

    w1_in, w1_out, w2_in, w2_out, w_mix_out = (_bf(w) for w in (ffn1_w_in, ffn1_w_out, ffn2_w_in, ffn2_w_out,
                                                                  mix_w_out))
    w_mix_in = _mix_weights(mix_w_in)

    x = x.reshape(n, d)
    for l in range(depth):
        x = _ffn(x, l, row(ffn1_norm[l]), w1_in, w1_out, tm=tm_ffn)
        pa, pb, pc, pd = _mix_proj(x, l, row(mix_norm[l]), w_mix_in, tm=tm)
        shp = lambda a: a.reshape(bsz, seq, a.shape[-1])
        ya, yb, yd = _recurrent_mixers(
            [_hgrn_io(shp(pa), hgrn_lb_logits.astype(F32), row(hgrn_out_norm[l]), ctab, gmat, bd, l, rows),
             _gdn_io(shp(pb), gdn_conv_w[l].astype(F32), rep(gdn_a_log[l]), rep(gdn_dt_bias[l]),
                     row(gdn_out_norm[l]), ctab, gmat, bd, rows),
             _ret_io(shp(pd), ret_tables, row(ret_out_norm[l]), gmat, rows)],
            bsz, seq, rows, "recurrent_mixers")
        yc = _sb_mixer(shp(pc), row(sb_out_norm[l]), u2, gmat)
        ys = [y.reshape(n, GROUP_W) for y in (ya, yb, yc, yd)]
        x = _ffn(x, l, row(ffn2_norm[l]), w2_in, w2_out, tm=tm_ffn, mix=(ys, w_mix_out),
                 final_gain=row(final_norm) if l == depth - 1 else None)
    return x.reshape(bsz, seq, d)
```

```python
import functools
import math

import numpy as np
import jax
import jax.numpy as jnp
from jax import lax
from jax.experimental import pallas as pl
from jax.experimental.pallas import tpu as pltpu

F32 = jnp.float32
BF16 = jnp.bfloat16

EPS = 1e-6
CHUNK = 64
N_HEADS = 4
HEAD_W = 64
GROUP_W = N_HEADS * HEAD_W
D_DK = 32
ROPE_BASE = 10000.0
N_LEVELS = 6
VMEM_LIMIT = 56 * 1024 * 1024

MIXER_ROWS = 512
GDN_GROUP = 4
FFN_COLS = 256
RET_CHUNK = 128
SB_BLOCK = 256
SB_UNIT = 256
SB_DEAD = -151.0
LOG2_E = 1.4426950408889634


def _bf(x):
    return x.astype(BF16)


def _dot(a, b):
    return jnp.dot(a, b, preferred_element_type=F32)


def _dot_nt(a, b):
    return lax.dot_general(a, b, (((1,), (1,)), ((), ())), preferred_element_type=F32)


def _dot_tn(a, b):
    return lax.dot_general(a, b, (((0,), (0,)), ((), ())), preferred_element_type=F32)


def _split2(x):
    hi = _bf(x)
    lo = _bf(x - hi.astype(F32))
    return hi, lo


def _split3(x):
    h1 = _bf(x)
    r = x - h1.astype(F32)
    h2 = _bf(r)
    h3 = _bf(r - h2.astype(F32))
    return h1, h2, h3


def _select_rows(w3, x):
    return _dot(w3, jnp.concatenate(_split3(x), axis=0))


def _headsum(x, gmat):
    hi, lo = _split2(x)
    return _dot(hi, gmat) + _dot(lo, gmat)


def _head_rmsnorm(o, gmat, gain):
    ms = _headsum(o * o, gmat) * (1.0 / HEAD_W)
    return o * lax.rsqrt(ms + EPS) * gain


def _sigmoid(x):
    return 0.5 * jnp.tanh(0.5 * x) + 0.5


def _sigmoid_rel(x):
    return jnp.exp(jnp.minimum(x, 0.0) - jnp.log(1.0 + jnp.exp(-jnp.abs(x))))


def _silu(x):
    return x * _sigmoid(x)


def _softplus(x):
    return jnp.maximum(x, 0.0) + jnp.log(1.0 + jnp.exp(-jnp.abs(x)))


def _head_masks(width, lanes_per_head, dtype):
    lane = lax.broadcasted_iota(jnp.int32, (1, width), 1)
    return [((lane // lanes_per_head) == h).astype(dtype) for h in range(N_HEADS)]


def _stack_heads(x, masks):
    return jnp.concatenate([x * m for m in masks], axis=0)


def _level_masks(lane_stack=1):
    r = lax.broadcasted_iota(jnp.int32, (CHUNK, lane_stack * CHUNK), 0)
    c = lax.broadcasted_iota(jnp.int32, (CHUNK, lane_stack * CHUNK), 1) & (CHUNK - 1)
    pair = []
    for l in range(N_LEVELS):
        n = CHUNK >> l
        m = n // 2
        sh = N_LEVELS - l
        same = (r >> sh) == (c >> sh)
        pair.append((same & ((r & (n - 1)) >= m) & ((c & (n - 1)) < m)).astype(F32))
    eye = (r == c).astype(F32)
    return pair, eye


def _decay_factors(cs):
    return [(jnp.exp(cs[(2 + l) * CHUNK:(3 + l) * CHUNK]),
             jnp.exp(cs[(2 + N_LEVELS + l) * CHUNK:(3 + N_LEVELS + l) * CHUNK])) for l in range(N_LEVELS)]


def _cumsum_table():
    t = np.arange(CHUNK)
    tri = (t[None, :] <= t[:, None]).astype(np.float32)
    up = (t[None, :] > t[:, None]).astype(np.float32)
    q_blocks, k_blocks = [], []
    for l in range(N_LEVELS):
        n = CHUNK >> l
        m = n // 2
        anchor = (t // n) * n + m - 1
        upper = ((t % n) >= m)[:, None]
        q_blocks.append(np.where(upper, tri - tri[anchor], 0.0))
        k_blocks.append(np.where(~upper, tri[anchor] - tri, 0.0))
    tab = np.concatenate([tri, up] + q_blocks + k_blocks, axis=0)
    assert set(np.unique(tab)) <= {0.0, 1.0}
    return jnp.asarray(np.concatenate([tab, tab, tab], axis=1), dtype=BF16)


def _head_block_matrix(rows, rows_per_head, cols, cols_per_head, dtype):
    r = np.arange(rows)[:, None] // rows_per_head
    c = np.arange(cols)[None, :] // cols_per_head
    return jnp.asarray((r == c).astype(np.float32), dtype=dtype)


def _ret_lane_head(p):
    return (p % 64) // 16


def _retention_tables(seq, chunk):
    p = np.arange(128)
    half = D_DK // 2
    inv_freq = ROPE_BASE ** (-(p % 16).astype(np.float64) / half)
    ang = np.arange(seq, dtype=np.float64)[:, None] * inv_freq[None, :]
    cos = np.cos(ang)
    sin = np.sin(ang) * np.where(p < 64, -1.0, 1.0)[None, :]
    log_gamma = np.log(1.0 - 2.0 ** (-5.0 - np.arange(N_HEADS, dtype=np.float64)))
    c = np.arange(chunk, dtype=np.float64)
    rel = c[:, None] - c[None, :]
    decay = np.where(rel[None] >= 0, np.exp(rel[None] * log_gamma[:, None, None]), 0.0)
    decay = decay.transpose(1, 0, 2).reshape(chunk, N_HEADS * chunk)
    lane_h = _ret_lane_head(p)
    zeta = np.exp((chunk - 1 - c)[:, None] * log_gamma[lane_h][None, :])
    xi = np.exp((c + 1.0)[:, None] * log_gamma[lane_h][None, :])
    gamma_c = np.exp(chunk * log_gamma)[np.arange(GROUP_W) // HEAD_W][None, :]
    bd = (lane_h[:, None] == (np.arange(GROUP_W) // HEAD_W)[None, :]).astype(np.float32)
    f = lambda a: jnp.asarray(a, dtype=F32)
    return f(cos), f(sin), f(decay), f(zeta), f(xi), f(gamma_c), f(bd)


def _ret_perm():
    p = np.arange(128)
    h = _ret_lane_head(p)
    return h * D_DK + (p % 16) + np.where(p >= 64, 16, 0)


def _rms_rows(x, gain):
    return x * lax.rsqrt(jnp.mean(x * x, axis=-1, keepdims=True) + EPS) * gain


def _ffn_body(has_mix, has_final, *refs):
    refs = list(refs)
    x_ref = refs.pop(0)
    y_refs = [refs.pop(0) for _ in range(4)] if has_mix else []
    wmix_ref = refs.pop(0) if has_mix else None
    g_ref, win_ref, wout_ref = refs.pop(0), refs.pop(0), refs.pop(0)
    gf_ref = refs.pop(0) if has_final else None
    o_ref = refs.pop(0)

    x = x_ref[...]
    for m, y_ref in enumerate(y_refs):
        x = x + _dot(y_ref[...], wmix_ref[m * GROUP_W:(m + 1) * GROUP_W, :])
    h = _bf(_rms_rows(x, g_ref[...]))

    d_ff = wout_ref.shape[0]
    steps = d_ff // FFN_COLS
    acc = None
    act = None
    for c in range(steps + 1):
        if c < steps:
            gate = _dot(h, win_ref[:, c * FFN_COLS:(c + 1) * FFN_COLS])
            up = _dot(h, win_ref[:, d_ff + c * FFN_COLS:d_ff + (c + 1) * FFN_COLS])
        if act is not None:
            down = _dot(act, wout_ref[(c - 1) * FFN_COLS:c * FFN_COLS, :])
            acc = down if acc is None else acc + down
        if c < steps:
            act = _bf(_silu(gate) * up)
    y = x + 0.5 * acc
    if has_final:
        y = _rms_rows(y, gf_ref[...])
    o_ref[...] = y


def _layer_spec(a, layer):
    return pl.BlockSpec((None,) + a.shape[1:], lambda i: (layer, 0, 0), pipeline_mode=pl.Buffered(1))


def _ffn(x, layer, gain, w_in, w_out, *, tm, mix=None, final_gain=None):
    n, d = x.shape
    rows = lambda w: pl.BlockSpec((tm, w), lambda i: (i, 0))
    whole = lambda a: _layer_spec(a, layer) if a.ndim == 3 else pl.BlockSpec(a.shape, lambda i: (0, 0))
    args, specs = [x], [rows(d)]
    if mix is not None:
        ys, w_mix = mix
        args += list(ys) + [w_mix]
        specs += [rows(GROUP_W) for _ in ys] + [whole(w_mix)]
    args += [gain, w_in, w_out]
    specs += [whole(gain), whole(w_in), whole(w_out)]
    if final_gain is not None:
        args.append(final_gain)
        specs.append(whole(final_gain))
    return pl.pallas_call(
        functools.partial(_ffn_body, mix is not None, final_gain is not None),
        grid=(n // tm,),
        in_specs=specs,
        out_specs=rows(d),
        out_shape=jax.ShapeDtypeStruct((n, d), F32),
        compiler_params=pltpu.CompilerParams(
            dimension_semantics=("parallel",), vmem_limit_bytes=VMEM_LIMIT),
        name="ffn",
    )(*args)


def _proj_body(x_ref, g_ref, wa_ref, wb_ref, wc_ref, wd_ref, oa_ref, ob_ref, oc_ref, od_ref):
    h = _bf(_rms_rows(x_ref[...], g_ref[...]))
    oa_ref[...] = _dot(h, wa_ref[...])
    ob_ref[...] = _dot(h, wb_ref[...])
    oc_ref[...] = _bf(_dot(h, wc_ref[...]))
    od_ref[...] = _dot(h, wd_ref[...])


def _mix_proj(x, layer, gain, ws, *, tm):
    n, d = x.shape
    dts = (F32, F32, BF16, F32)
    return pl.pallas_call(
        _proj_body,
        grid=(n // tm,),
        in_specs=[pl.BlockSpec((tm, d), lambda i: (i, 0)), pl.BlockSpec((1, d), lambda i: (0, 0))]
        + [_layer_spec(w, layer) for w in ws],
        out_specs=[pl.BlockSpec((tm, w.shape[2]), lambda i: (i, 0)) for w in ws],
        out_shape=[jax.ShapeDtypeStruct((n, w.shape[2]), dt) for w, dt in zip(ws, dts)],
        compiler_params=pltpu.CompilerParams(
            dimension_semantics=("parallel",), vmem_limit_bytes=VMEM_LIMIT),
        name="mix_proj",
    )(x, gain, *ws)


def _hgrn_steps(layer, q_ref, f_ref, i_ref, g_ref, lbl_ref, ng_ref, ctab_ref, gmat_ref, bd_ref,
                o_ref, st_scr):
    @pl.when(pl.program_id(1) == 0)
    def _():
        st_scr[...] = jnp.zeros_like(st_scr)

    logits = lbl_ref[...]
    e = jnp.exp(logits - jnp.max(logits, axis=0, keepdims=True))
    p = e / jnp.sum(e, axis=0, keepdims=True)
    lb = jnp.zeros_like(p[0:1])
    for r in range(1, layer + 1):
        lb = lb + p[r:r + 1]

    pair, eye = _level_masks(lane_stack=N_HEADS)
    hm = _head_masks(GROUP_W, HEAD_W, BF16)
    ctab = ctab_ref[...]
    gmat = gmat_ref[...]
    bd = bd_ref[...]
    ng = ng_ref[...]

    chunks = range(q_ref.shape[1] // CHUNK)
    rows_of = lambda c: pl.ds(c * CHUNK, CHUNK)
    qs, kk, cs = [], [], []
    for c in chunks:
        f = lb + (1.0 - lb) * _sigmoid_rel(f_ref[0, rows_of(c), :])
        qs.append(_silu(q_ref[0, rows_of(c), :]))
        kk.append(1.0 - f)
        cs.append(_select_rows(ctab, jnp.log(f)))
        yield 250

    scores = []
    for c in chunks:
        sc = eye * _dot_nt(_bf(qs[c]), _stack_heads(_bf(kk[c]), hm))
        for l, (fq, fk) in enumerate(_decay_factors(cs[c])):
            sc = sc + pair[l] * _dot_nt(_bf(qs[c] * fq), _stack_heads(_bf(kk[c] * fk), hm))
        scores.append(_bf(sc))
        yield 450

    qdec, b_last, o_intra, st_add = [], [], [], []
    for c in chunks:
        b = cs[c][0:CHUNK]
        rem = cs[c][CHUNK:2 * CHUNK]
        v = _bf(_silu(i_ref[0, rows_of(c), :]))
        o_intra.append(_dot(scores[c], _stack_heads(v, hm)))
        qdec.append(_bf(qs[c] * jnp.exp(b)))
        st_add.append(_dot_tn(v, _bf(kk[c] * jnp.exp(rem))) * bd)
        b_last.append(jnp.exp(b[CHUNK - 1:CHUNK]))
        yield 200

    st = st_scr[...]
    outs = []
    for c in chunks:
        outs.append(o_intra[c] + _dot_nt(qdec[c], _bf(st)))
        st = st * b_last[c] + st_add[c]
        yield 100
    st_scr[...] = st
    o_ref[0] = _bf(_head_rmsnorm(jnp.concatenate(outs, axis=0), gmat, ng) * _sigmoid(g_ref[0]))
    yield 150


def _weave(gens):
    spent = [0.0] * len(gens)
    live = list(range(len(gens)))
    while live:
        i = min(live, key=lambda j: spent[j])
        try:
            spent[i] += next(gens[i])
        except StopIteration:
            live.remove(i)


def _whole_spec(a):
    return pl.BlockSpec(a.shape, lambda b, t: (0,) * a.ndim)


def _col_spec(rows, width, c):
    return pl.BlockSpec((1, rows, width), lambda b, t: (b, t, c))


def _recurrent_mixers(mixers, bsz, seq, rows, name):
    n_in = [len(m[1]) for m in mixers]
    n_scr = [len(m[3]) for m in mixers]

    def body(*refs):
        ins, outs, scr = refs[:sum(n_in)], refs[sum(n_in):sum(n_in) + len(mixers)], refs[sum(n_in) + len(mixers):]
        gens = []
        for i, m in enumerate(mixers):
            a, s = sum(n_in[:i]), sum(n_scr[:i])
            gens.append(m[0](*ins[a:a + n_in[i]], outs[i], *scr[s:s + n_scr[i]]))
        _weave(gens)

    out = pl.pallas_call(
        body,
        grid=(bsz, seq // rows),
        in_specs=[s for m in mixers for s in m[2]],
        out_specs=[pl.BlockSpec((1, rows, GROUP_W), lambda b, t: (b, t, 0)) for _ in mixers],
        out_shape=[jax.ShapeDtypeStruct((bsz, seq, GROUP_W), BF16) for _ in mixers],
        scratch_shapes=[s for m in mixers for s in m[3]],
        compiler_params=pltpu.CompilerParams(
            dimension_semantics=("parallel", "arbitrary"), vmem_limit_bytes=VMEM_LIMIT),
        name=name,
    )(*[a for m in mixers for a in m[1]])
    return out


def _hgrn_io(proj_a, lb_logits, norm_g, ctab, gmat, bd, layer, rows):
    small = [lb_logits, norm_g, ctab, gmat, bd]
    return (functools.partial(_hgrn_steps, layer), [proj_a] * 4 + small,
            [_col_spec(rows, GROUP_W, c) for c in range(4)] + [_whole_spec(a) for a in small],
            [pltpu.VMEM((GROUP_W, GROUP_W), F32)])


def _hgrn_mixer(proj_a, lb_logits, norm_g, ctab, gmat, bd, layer, rows):
    bsz, seq, _ = proj_a.shape
    return _recurrent_mixers([_hgrn_io(proj_a, lb_logits, norm_g, ctab, gmat, bd, layer, rows)],
                             bsz, seq, rows, "hgrn2")[0]


def _gdn_steps(q_ref, k_ref, v_ref, g_ref, a_ref, bl_ref, cw_ref, alog_ref, dtb_ref, ng_ref,
               ctab_ref, gmat_ref, bd_ref, o_ref, s_scr, tail_scr, ext_scr):
    @pl.when(pl.program_id(1) == 0)
    def _():
        s_scr[...] = jnp.zeros_like(s_scr)
        tail_scr[...] = jnp.zeros_like(tail_scr)

    pair, eye = _level_masks()
    hm = _head_masks(GROUP_W, HEAD_W, BF16)
    hmf = _head_masks(GROUP_W, HEAD_W, F32)
    ctab = ctab_ref[...]
    gmat = gmat_ref[...]
    bd = bd_ref[...]
    ng = ng_ref[...]
    neg_a = -jnp.exp(alog_ref[...])
    dtb = dtb_ref[...]
    raw = (q_ref, k_ref, v_ref)
    rows = q_ref.shape[1]

    for j in range(3):
        ext_scr[0:8, j * GROUP_W:(j + 1) * GROUP_W] = tail_scr[:, j * GROUP_W:(j + 1) * GROUP_W]
        ext_scr[8:8 + CHUNK, j * GROUP_W:(j + 1) * GROUP_W] = raw[j][0, 0:CHUNK, :]
        tail_scr[:, j * GROUP_W:(j + 1) * GROUP_W] = raw[j][0, rows - 8:rows, :]

    def conv(c, j):
        w = cw_ref[:, j * GROUP_W:(j + 1) * GROUP_W]
        acc = None
        for d in range(4):
            if c == 0:
                x = ext_scr[pl.ds(8 - d, CHUNK), j * GROUP_W:(j + 1) * GROUP_W]
            else:
                x = raw[j][0, pl.ds(c * CHUNK - d, CHUNK), :]
            term = x * w[3 - d:4 - d, :]
            acc = term if acc is None else acc + term
        return _silu(acc)

    nch = rows // CHUNK
    rr = lax.broadcasted_iota(jnp.int32, (N_HEADS * CHUNK, CHUNK), 0) & (CHUNK - 1)
    cc = lax.broadcasted_iota(jnp.int32, (N_HEADS * CHUNK, CHUNK), 1)
    tril = rr >= cc
    tril_f = tril.astype(F32)
    stril_f = (rr > cc).astype(F32)
    sub = lax.broadcasted_iota(jnp.int32, (1, GROUP_W), 1) & (HEAD_W - 1)
    lane_is = [(sub == i).astype(BF16) for i in range(6)]
    qc, kc, vc, qn, kn, cs, diff, prod, kb, beta = ({} for _ in range(10))
    qg, kdec, vb, kbg, g_last, attn, m_mat, t_inv = ({} for _ in range(8))
    o_lhs, o_add, s_lhs, s_add = {}, {}, {}, {}

    def phase1(grp):
        for c in grp:
            qc[c] = conv(c, 0)
            kc[c] = conv(c, 1)
            vc[c] = conv(c, 2)
            qn[c] = _headsum(qc[c] * qc[c], gmat)
            kn[c] = _headsum(kc[c] * kc[c], gmat)
            log_alpha = neg_a * _softplus(a_ref[0, pl.ds(c * CHUNK, CHUNK), :] + dtb)
            cs[c] = _select_rows(ctab[0:2 * CHUNK], log_alpha)
            yield
        for c in grp:
            qc[c] = qc[c] * lax.rsqrt(qn[c] + EPS) * (HEAD_W ** -0.5)
            kc[c] = kc[c] * lax.rsqrt(kn[c] + EPS)
            beta[c] = _sigmoid(bl_ref[0, pl.ds(c * CHUNK, CHUNK), :])
            kb[c] = kc[c] * beta[c]
            g1, g2, g3 = _split3(cs[c][0:CHUNK])
            lhs = g1 * lane_is[0] + g2 * lane_is[1] + g3 * lane_is[2] + (lane_is[3] + lane_is[4] + lane_is[5])
            rhs = (lane_is[0] + lane_is[1] + lane_is[2]) - g1 * lane_is[3] - g2 * lane_is[4] - g3 * lane_is[5]
            diff[c] = _dot_nt(_stack_heads(lhs, hm), rhs)
            prod[c] = _dot_nt(jnp.concatenate([_stack_heads(_bf(qc[c]), hm), _stack_heads(_bf(kb[c]), hm)],
                                              axis=0), _bf(kc[c]))
            yield
        for c in grp:
            gc = cs[c][0:CHUNK]
            rem = cs[c][CHUNK:2 * CHUNK]
            dec = jnp.exp(jnp.where(tril, diff[c], 0.0))
            attn_st = _bf(tril_f * dec * prod[c][0:N_HEADS * CHUNK])
            m_st = stril_f * dec * prod[c][N_HEADS * CHUNK:2 * N_HEADS * CHUNK]
            for h in range(N_HEADS):
                attn[c, h] = attn_st[h * CHUNK:(h + 1) * CHUNK]
                m_mat[c, h] = m_st[h * CHUNK:(h + 1) * CHUNK]
            qg[c] = qc[c] * jnp.exp(gc)
            kdec[c] = _bf(kc[c] * jnp.exp(rem))
            vb[c] = vc[c] * beta[c]
            kbg[c] = kb[c] * jnp.exp(gc)
            g_last[c] = jnp.exp(gc[CHUNK - 1:CHUNK])
            yield

    def phase2(grp):
        pairs = [(c, h) for c in grp for h in range(N_HEADS)]
        for p in pairs:
            t_inv[p] = eye - pair[N_LEVELS - 1] * m_mat[p]
        for l in range(N_LEVELS - 2, -1, -1):
            t_b = {p: _bf(t_inv[p]) for p in pairs}
            tl = {p: _dot(t_b[p], _bf(pair[l] * m_mat[p])) for p in pairs}
            yield
            for p in pairs:
                t_inv[p] = t_inv[p] - _dot(_bf(tl[p]), t_b[p])
            yield

    def phase3(grp):
        uw = {}
        for c in grp:
            rhs = jnp.concatenate([vb[c], kbg[c]], axis=1)
            acc = None
            for h in range(N_HEADS):
                lhs = jnp.concatenate(_split2(t_inv[c, h]), axis=1)
                rhs_h = _bf(rhs * jnp.concatenate([hmf[h], hmf[h]], axis=1))
                term = _dot(lhs, jnp.concatenate([rhs_h, rhs_h], axis=0))
                acc = term if acc is None else acc + term
            uw[c] = _bf(acc)
            yield
        for c in grp:
            u_b = uw[c][:, 0:GROUP_W]
            w_b = uw[c][:, GROUP_W:2 * GROUP_W]
            aw = None
            for h in range(N_HEADS):
                term = _dot(attn[c, h], jnp.concatenate([u_b * hm[h], w_b * hm[h]], axis=1))
                aw = term if aw is None else aw + term
            o_add[c] = aw[:, 0:GROUP_W]
            o_lhs[c] = _bf(qg[c] - aw[:, GROUP_W:2 * GROUP_W])
            ks = _dot_tn(kdec[c], uw[c])
            s_add[c] = ks[:, 0:GROUP_W] * bd
            s_lhs[c] = _bf(-ks[:, GROUP_W:2 * GROUP_W] * bd)
            yield

    state = [s_scr[...]]
    outs = []

    def phase4(grp):
        for c in grp:
            s_b = _bf(state[0])
            outs.append(o_add[c] + _dot(o_lhs[c], s_b))
            state[0] = state[0] * g_last[c] + _dot(s_lhs[c], s_b) + s_add[c]
            yield

    def chain(*gens):
        for gen in gens:
            yield from gen

    def run(primary, secondary=iter(()), per_step=1):
        for _ in primary:
            for _ in range(per_step):
                next(secondary, None)
            yield 400
        for _ in secondary:
            yield 200

    groups = [range(g, min(g + GDN_GROUP, nch)) for g in range(0, nch, GDN_GROUP)]
    yield from run(phase1(groups[0]))
    for gi, grp in enumerate(groups):
        fill = []
        if gi > 0:
            fill += [phase3(groups[gi - 1]), phase4(groups[gi - 1])]
        if gi + 1 < len(groups):
            fill.append(phase1(groups[gi + 1]))
        yield from run(phase2(grp), chain(*fill), per_step=2)
    yield from run(chain(phase3(groups[-1]), phase4(groups[-1])))
    s_scr[...] = state[0]
    o_ref[0] = _bf(_head_rmsnorm(jnp.concatenate(outs, axis=0), gmat, ng) * _silu(g_ref[0]))
    yield 200


def _gdn_io(proj_b, conv_w, a_log_rep, dt_bias_rep, norm_g, ctab, gmat, bd, rows):
    small = [conv_w, a_log_rep, dt_bias_rep, norm_g, ctab, gmat, bd]
    return (_gdn_steps, [proj_b] * 6 + small,
            [_col_spec(rows, GROUP_W, c) for c in range(6)] + [_whole_spec(a) for a in small],
            [pltpu.VMEM((GROUP_W, GROUP_W), F32),
             pltpu.VMEM((8, 3 * GROUP_W), F32),
             pltpu.VMEM((8 + CHUNK, 3 * GROUP_W), F32)])


def _gdn_mixer(proj_b, conv_w, a_log_rep, dt_bias_rep, norm_g, ctab, gmat, bd, rows):
    bsz, seq, _ = proj_b.shape
    return _recurrent_mixers([_gdn_io(proj_b, conv_w, a_log_rep, dt_bias_rep, norm_g, ctab, gmat, bd, rows)],
                             bsz, seq, rows, "gdn")[0]


def _sb_steps(q_ref, k_ref, v_ref, ng_ref, later_ref, gmat_ref, o_ref):
    blk, unit = SB_BLOCK, SB_UNIT
    assert blk == unit
    n_blk = q_ref.shape[1] // blk
    hm = _head_masks(GROUP_W, HEAD_W, BF16)
    later = later_ref[...]
    t_loc = lax.broadcasted_iota(jnp.int32, (N_HEADS * blk, unit), 0) & (blk - 1)
    s_loc = lax.broadcasted_iota(jnp.int32, (N_HEADS * blk, unit), 1)
    past0 = t_loc > s_loc

    def scores(qs, u, past):
        kblk = k_ref[0, pl.ds(u * unit, unit), :]
        z2 = _dot_nt(qs, kblk) * LOG2_E
        log_sig = jnp.minimum(z2, 0.0) - jnp.log2(1.0 + jnp.exp2(-jnp.abs(z2)))
        log_stay = log_sig - z2
        if past is not None:
            log_stay = jnp.where(past, log_stay, 0.0)
        sfx = _dot(_bf(log_stay), later)
        return log_sig + sfx, jnp.sum(log_stay, axis=-1, keepdims=True)

    def weights(arg, carry, past):
        wts = jnp.exp2(arg + carry)
        if past is not None:
            wts = jnp.where(past, wts, 0.0)
        wts = _bf(wts)
        return jnp.concatenate([wts[h * blk:(h + 1) * blk] for h in range(N_HEADS)], axis=1)

    def values(u, scale=None):
        vblk = v_ref[0, pl.ds(u * unit, unit), :]
        if scale is not None:
            vblk = vblk * scale
        return jnp.concatenate([vblk * hm[h] for h in range(N_HEADS)], axis=0)

    started = []
    for j in range(n_blk):
        u0 = pl.program_id(1) * n_blk + j
        q = q_ref[0, j * blk:(j + 1) * blk, :] * (HEAD_W ** -0.5)
        qs = jnp.concatenate([q * hm[h] for h in range(N_HEADS)], axis=0)
        has_prev = (u0 > 0).astype(F32)
        arg0, total0 = scores(qs, u0, past0)
        yield 900
        arg1, total1 = scores(qs, jnp.maximum(u0 - 1, 0), None)
        yield 900
        w0 = weights(arg0, 0.0, past0)
        w1 = weights(arg1, total0, None)
        yield 500
        v1 = values(jnp.maximum(u0 - 1, 0), jnp.broadcast_to(has_prev, (1, GROUP_W)).astype(BF16))
        acc = _dot(jnp.concatenate([w0, w1], axis=1), jnp.concatenate([values(u0), v1], axis=0))
        started.append((qs, u0, total0 + total1 * has_prev, acc))
        yield 500

    yield float("inf")

    for j, (qs, u0, carry, acc) in enumerate(started):
        def live(state):
            return jnp.logical_and(state[0] >= 0, state[1])

        def step(state, qs=qs):
            u, _, carry, acc = state
            arg, total = scores(qs, u, None)
            acc = acc + _dot(weights(arg, carry, None), values(u))
            carry = carry + total
            return u - 1, jnp.max(carry) > SB_DEAD, carry, acc

        _, _, carry, acc = lax.while_loop(live, step, (u0 - 2, jnp.max(carry) > SB_DEAD, carry, acc))
        o_ref[0, j * blk:(j + 1) * blk, :] = _bf(_head_rmsnorm(acc, gmat_ref[...], ng_ref[...]))


def _sb_io(proj_c, norm_g, later, gmat, rows):
    seq = proj_c.shape[1]
    small = [norm_g, later, gmat]
    return (_sb_steps, [proj_c] * 3 + small,
            [_col_spec(rows, GROUP_W, 0),
             pl.BlockSpec((1, seq, GROUP_W), lambda b, t: (b, 0, 1)),
             pl.BlockSpec((1, seq, GROUP_W), lambda b, t: (b, 0, 2))] + [_whole_spec(a) for a in small],
            [])


def _sb_mixer(proj_c, norm_g, later, gmat, rows=SB_BLOCK):
    bsz, seq, _ = proj_c.shape
    return _recurrent_mixers([_sb_io(proj_c, norm_g, later, gmat, rows)], bsz, seq, rows, "stickbreak")[0]


def _sb_table():
    j = np.arange(SB_UNIT)[:, None]
    s = np.arange(SB_UNIT)[None, :]
    return jnp.asarray((j > s).astype(np.float32), dtype=BF16)


def _ret_steps(q_ref, k_ref, v_ref, g_ref, cos_ref, sin_ref, decay_ref, zeta_ref, xi_ref, gc_ref,
               bd_ref, ng_ref, gmat_ref, o_ref, s_scr):
    @pl.when(pl.program_id(1) == 0)
    def _():
        s_scr[...] = jnp.zeros_like(s_scr)

    chunk = decay_ref.shape[0]
    decay = decay_ref[...]
    lane = lax.broadcasted_iota(jnp.int32, (1, 2 * HEAD_W), 1)
    hq = [(((lane % 64) // 16) == h).astype(BF16) for h in range(N_HEADS)]
    hv = _head_masks(GROUP_W, HEAD_W, BF16)
    gmat = gmat_ref[...]
    ng = ng_ref[...]
    bd = bd_ref[...]
    zeta = zeta_ref[...]
    xi = xi_ref[...]
    gamma_c = gc_ref[...]

    nch = q_ref.shape[1] // chunk
    qx, o_intra, s_add = [], [], []
    for c in range(nch):
        sl = pl.ds(c * chunk, chunk)
        cos = cos_ref[sl, :]
        sin = sin_ref[sl, :]
        q = q_ref[0, sl, :]
        k = k_ref[0, sl, :]
        qr = q * cos + pltpu.roll(q, 64, axis=1) * sin
        kr = (k * cos + pltpu.roll(k, 64, axis=1) * sin) * (D_DK ** -0.5)
        v = _bf(v_ref[0, sl, :])
        sc = _bf(_dot_nt(_bf(qr), _stack_heads(_bf(kr), hq)) * decay)
        o_intra.append(_dot(sc, _stack_heads(v, hv)))
        qx.append(_bf(qr * xi))
        s_add.append(_dot_tn(_bf(kr * zeta), v) * bd)
        yield 350

    s = s_scr[...]
    outs = []
    for c in range(nch):
        outs.append(o_intra[c] + _dot(qx[c], _bf(s)))
        s = s * gamma_c + s_add[c]
        yield 80
    s_scr[...] = s
    o_ref[0] = _bf(_head_rmsnorm(jnp.concatenate(outs, axis=0), gmat, ng) * _silu(g_ref[0]))
    yield 150


def _ret_io(proj_d, tables, norm_g, gmat, rows):
    cos, sin, decay, zeta, xi, gamma_c, bd = tables
    small = [decay, zeta, xi, gamma_c, bd, norm_g, gmat]
    return (_ret_steps, [proj_d] * 4 + [cos, sin] + small,
            [_col_spec(rows, 128, 0), _col_spec(rows, 128, 1), _col_spec(rows, GROUP_W, 1),
             _col_spec(rows, GROUP_W, 2),
             pl.BlockSpec((rows, 128), lambda b, t: (t, 0)), pl.BlockSpec((rows, 128), lambda b, t: (t, 0))]
            + [_whole_spec(a) for a in small],
            [pltpu.VMEM((2 * HEAD_W, GROUP_W), F32)])


def _ret_mixer(proj_d, tables, norm_g, gmat, rows):
    bsz, seq, _ = proj_d.shape
    return _recurrent_mixers([_ret_io(proj_d, tables, norm_g, gmat, rows)], bsz, seq, rows, "retention")[0]


def _mix_weights(w_in):
    w_in = _bf(w_in)
    a_w = w_in[..., 0:1024]
    b0 = 1024
    b_main = w_in[..., b0:b0 + 1024]
    b_a = jnp.repeat(w_in[..., b0 + 1024:b0 + 1028], HEAD_W, axis=-1)
    b_b = jnp.repeat(w_in[..., b0 + 1028:b0 + 1032], HEAD_W, axis=-1)
    c0 = b0 + 1032
    c_w = w_in[..., c0:c0 + 768]
    d0 = c0 + 768
    perm = _ret_perm()
    d_q = w_in[..., d0:d0 + 128][..., perm]
    d_k = w_in[..., d0 + 128:d0 + 256][..., perm]
    d_rest = w_in[..., d0 + 256:d0 + 768]
    b_w = jnp.concatenate([b_main, b_a, b_b], axis=-1)
    d_w = jnp.concatenate([d_q, d_k, d_rest], axis=-1)
    return a_w, b_w, c_w, d_w


def kernel(x, ffn1_norm, ffn1_w_in, ffn1_w_out, mix_norm, mix_w_in, mix_w_out, ffn2_norm, ffn2_w_in,
           ffn2_w_out, hgrn_lb_logits, hgrn_out_norm, gdn_conv_w, gdn_a_log, gdn_dt_bias, gdn_out_norm,
           sb_out_norm, ret_out_norm, final_norm):
    bsz, seq, d = x.shape
    depth = ffn1_norm.shape[0]
    n = bsz * seq
    d_ff = ffn1_w_out.shape[1]
    tm_ffn = 1024
    tm = 512
    rows = MIXER_ROWS

    ctab = _cumsum_table()
    gmat = _head_block_matrix(GROUP_W, HEAD_W, GROUP_W, HEAD_W, BF16)
    bd = _head_block_matrix(GROUP_W, HEAD_W, GROUP_W, HEAD_W, F32)
    u2 = _sb_table()
    ret_tables = _retention_tables(seq, RET_CHUNK)
    row = lambda a: a.reshape(1, -1).astype(F32)
    rep = lambda a: jnp.repeat(a.astype(F32), HEAD_W).reshape(1, GROUP_W)

    w1_in, w1_out, w2_in, w2_out, w_mix_out = (_bf(w) for w in (ffn1_w_in, ffn1_w_out, ffn2_w_in, ffn2_w_out,
                                                                  mix_w_out))
    w_mix_in = _mix_weights(mix_w_in)

    x = x.reshape(n, d)
    for l in range(depth):
        x = _ffn(x, l, row(ffn1_norm[l]), w1_in, w1_out, tm=tm_ffn)
        pa, pb, pc, pd = _mix_proj(x, l, row(mix_norm[l]), w_mix_in, tm=tm)
        shp = lambda a: a.reshape(bsz, seq, a.shape[-1])
        ya, yb, yc, yd = _recurrent_mixers(
            [_hgrn_io(shp(pa), hgrn_lb_logits.astype(F32), row(hgrn_out_norm[l]), ctab, gmat, bd, l, rows),
             _gdn_io(shp(pb), gdn_conv_w[l].astype(F32), rep(gdn_a_log[l]), rep(gdn_dt_bias[l]),
                     row(gdn_out_norm[l]), ctab, gmat, bd, rows),
             _sb_io(shp(pc), row(sb_out_norm[l]), u2, gmat, rows),
             _ret_io(shp(pd), ret_tables, row(ret_out_norm[l]), gmat, rows)],
            bsz, seq, rows, "mixers")
        ys = [y.reshape(n, GROUP_W) for y in (ya, yb, yc, yd)]
        x = _ffn(x, l, row(ffn2_norm[l]), w2_in, w2_out, tm=tm_ffn, mix=(ys, w_mix_out),
                 final_gain=row(final_norm) if l == depth - 1 else None)
    return x.reshape(bsz, seq, d)
```

```python
import functools

import numpy as np
import jax
import jax.numpy as jnp
from jax import lax
from jax.experimental import pallas as pl
from jax.experimental.pallas import tpu as pltpu

F32 = jnp.float32
BF16 = jnp.bfloat16

EPS = 1e-6
CHUNK = 64
N_HEADS = 4
HEAD_W = 64
GROUP_W = N_HEADS * HEAD_W
D_DK = 32
ROPE_BASE = 10000.0
N_LEVELS = 6
VMEM_LIMIT = 56 * 1024 * 1024

FFN_ROWS = 1024
PROJ_ROWS = 512
MIXER_ROWS = 512
GDN_GROUP = 4
FFN_COLS = 256
RET_CHUNK = 128
SB_BLOCK = 256
SB_UNIT = 256
SB_DEAD = -151.0
LOG2_E = 1.4426950408889634


def _bf(x):
    return x.astype(BF16)


def _dot(a, b):
    return jnp.dot(a, b, preferred_element_type=F32)


def _dot_nt(a, b):
    return lax.dot_general(a, b, (((1,), (1,)), ((), ())), preferred_element_type=F32)


def _dot_tn(a, b):
    return lax.dot_general(a, b, (((0,), (0,)), ((), ())), preferred_element_type=F32)


def _split2(x):
    hi = _bf(x)
    lo = _bf(x - hi.astype(F32))
    return hi, lo


def _split3(x):
    h1 = _bf(x)
    r = x - h1.astype(F32)
    h2 = _bf(r)
    h3 = _bf(r - h2.astype(F32))
    return h1, h2, h3


def _select_rows(w3, x):
    return _dot(w3, jnp.concatenate(_split3(x), axis=0))


def _headsum(x, gmat):
    hi, lo = _split2(x)
    return _dot(hi, gmat) + _dot(lo, gmat)


def _head_rmsnorm(o, gmat, gain):
    ms = _headsum(o * o, gmat) * (1.0 / HEAD_W)
    return o * lax.rsqrt(ms + EPS) * gain


def _sigmoid(x):
    return 0.5 * jnp.tanh(0.5 * x) + 0.5


def _sigmoid_rel(x):
    return jnp.exp(jnp.minimum(x, 0.0) - jnp.log(1.0 + jnp.exp(-jnp.abs(x))))


def _silu(x):
    return x * _sigmoid(x)


def _softplus(x):
    return jnp.maximum(x, 0.0) + jnp.log(1.0 + jnp.exp(-jnp.abs(x)))


def _head_masks(width, lanes_per_head, dtype):
    lane = lax.broadcasted_iota(jnp.int32, (1, width), 1)
    return [((lane // lanes_per_head) == h).astype(dtype) for h in range(N_HEADS)]


def _stack_heads(x, masks):
    return jnp.concatenate([x * m for m in masks], axis=0)


def _level_masks(lane_stack=1):
    r = lax.broadcasted_iota(jnp.int32, (CHUNK, lane_stack * CHUNK), 0)
    c = lax.broadcasted_iota(jnp.int32, (CHUNK, lane_stack * CHUNK), 1) & (CHUNK - 1)
    pair = []
    for l in range(N_LEVELS):
        n = CHUNK >> l
        m = n // 2
        sh = N_LEVELS - l
        same = (r >> sh) == (c >> sh)
        pair.append((same & ((r & (n - 1)) >= m) & ((c & (n - 1)) < m)).astype(F32))
    eye = (r == c).astype(F32)
    return pair, eye


def _decay_factors(cs):
    return [(jnp.exp(cs[(2 + l) * CHUNK:(3 + l) * CHUNK]),
             jnp.exp(cs[(2 + N_LEVELS + l) * CHUNK:(3 + N_LEVELS + l) * CHUNK])) for l in range(N_LEVELS)]


def _cumsum_table():
    t = np.arange(CHUNK)
    tri = (t[None, :] <= t[:, None]).astype(np.float32)
    up = (t[None, :] > t[:, None]).astype(np.float32)
    q_blocks, k_blocks = [], []
    for l in range(N_LEVELS):
        n = CHUNK >> l
        m = n // 2
        anchor = (t // n) * n + m - 1
        upper = ((t % n) >= m)[:, None]
        q_blocks.append(np.where(upper, tri - tri[anchor], 0.0))
        k_blocks.append(np.where(~upper, tri[anchor] - tri, 0.0))
    tab = np.concatenate([tri, up] + q_blocks + k_blocks, axis=0)
    assert set(np.unique(tab)) <= {0.0, 1.0}
    return jnp.asarray(np.concatenate([tab, tab, tab], axis=1), dtype=BF16)


def _head_block_matrix(rows, rows_per_head, cols, cols_per_head, dtype):
    r = np.arange(rows)[:, None] // rows_per_head
    c = np.arange(cols)[None, :] // cols_per_head
    return jnp.asarray((r == c).astype(np.float32), dtype=dtype)


def _ret_lane_head(p):
    return (p % 64) // 16


def _retention_tables(seq, chunk):
    p = np.arange(128)
    half = D_DK // 2
    inv_freq = ROPE_BASE ** (-(p % 16).astype(np.float64) / half)
    ang = np.arange(seq, dtype=np.float64)[:, None] * inv_freq[None, :]
    cos = np.cos(ang)
    sin = np.sin(ang) * np.where(p < 64, -1.0, 1.0)[None, :]
    log_gamma = np.log(1.0 - 2.0 ** (-5.0 - np.arange(N_HEADS, dtype=np.float64)))
    c = np.arange(chunk, dtype=np.float64)
    rel = c[:, None] - c[None, :]
    decay = np.where(rel[None] >= 0, np.exp(rel[None] * log_gamma[:, None, None]), 0.0)
    decay = decay.transpose(1, 0, 2).reshape(chunk, N_HEADS * chunk)
    lane_h = _ret_lane_head(p)
    zeta = np.exp((chunk - 1 - c)[:, None] * log_gamma[lane_h][None, :])
    xi = np.exp((c + 1.0)[:, None] * log_gamma[lane_h][None, :])
    gamma_c = np.exp(chunk * log_gamma)[np.arange(GROUP_W) // HEAD_W][None, :]
    bd = (lane_h[:, None] == (np.arange(GROUP_W) // HEAD_W)[None, :]).astype(np.float32)
    f = lambda a: jnp.asarray(a, dtype=F32)
    return f(cos), f(sin), f(decay), f(zeta), f(xi), f(gamma_c), f(bd)


def _ret_perm():
    p = np.arange(128)
    h = _ret_lane_head(p)
    return h * D_DK + (p % 16) + np.where(p >= 64, 16, 0)


def _rms_rows(x, gain):
    return x * lax.rsqrt(jnp.mean(x * x, axis=-1, keepdims=True) + EPS) * gain


def _ffn_body(has_mix, has_final, *refs):
    refs = list(refs)
    x_ref = refs.pop(0)
    y_refs = [refs.pop(0) for _ in range(4)] if has_mix else []
    wmix_ref = refs.pop(0) if has_mix else None
    g_ref, win_ref, wout_ref = refs.pop(0), refs.pop(0), refs.pop(0)
    gf_ref = refs.pop(0) if has_final else None
    o_ref = refs.pop(0)

    x = x_ref[...]
    for m, y_ref in enumerate(y_refs):
        x = x + _dot(y_ref[...], wmix_ref[m * GROUP_W:(m + 1) * GROUP_W, :])
    h = _bf(_rms_rows(x, g_ref[...]))

    d_ff = wout_ref.shape[0]
    steps = d_ff // FFN_COLS
    acc = None
    act = None
    for c in range(steps + 1):
        if c < steps:
            gate = _dot(h, win_ref[:, c * FFN_COLS:(c + 1) * FFN_COLS])
            up = _dot(h, win_ref[:, d_ff + c * FFN_COLS:d_ff + (c + 1) * FFN_COLS])
        if act is not None:
            down = _dot(act, wout_ref[(c - 1) * FFN_COLS:c * FFN_COLS, :])
            acc = down if acc is None else acc + down
        if c < steps:
            act = _bf(_silu(gate) * up)
    y = x + 0.5 * acc
    if has_final:
        y = _rms_rows(y, gf_ref[...])
    o_ref[...] = y


def _layer_spec(a, layer):
    return pl.BlockSpec((None,) + a.shape[1:], lambda i: (layer, 0, 0), pipeline_mode=pl.Buffered(1))


def _ffn(x, layer, gain, w_in, w_out, *, tm, mix=None, final_gain=None):
    n, d = x.shape
    rows = lambda w: pl.BlockSpec((tm, w), lambda i: (i, 0))
    whole = lambda a: _layer_spec(a, layer) if a.ndim == 3 else pl.BlockSpec(a.shape, lambda i: (0, 0))
    args, specs = [x], [rows(d)]
    if mix is not None:
        ys, w_mix = mix
        args += list(ys) + [w_mix]
        specs += [rows(GROUP_W) for _ in ys] + [whole(w_mix)]
    args += [gain, w_in, w_out]
    specs += [whole(gain), whole(w_in), whole(w_out)]
    if final_gain is not None:
        args.append(final_gain)
        specs.append(whole(final_gain))
    return pl.pallas_call(
        functools.partial(_ffn_body, mix is not None, final_gain is not None),
        grid=(n // tm,),
        in_specs=specs,
        out_specs=rows(d),
        out_shape=jax.ShapeDtypeStruct((n, d), F32),
        compiler_params=pltpu.CompilerParams(
            dimension_semantics=("parallel",), vmem_limit_bytes=VMEM_LIMIT),
        name="ffn",
    )(*args)


def _proj_body(x_ref, g_ref, wa_ref, wb_ref, wc_ref, wd_ref, oa_ref, ob_ref, oc_ref, od_ref):
    h = _bf(_rms_rows(x_ref[...], g_ref[...]))
    oa_ref[...] = _dot(h, wa_ref[...])
    ob_ref[...] = _dot(h, wb_ref[...])
    oc_ref[...] = _bf(_dot(h, wc_ref[...]))
    od_ref[...] = _dot(h, wd_ref[...])


def _mix_proj(x, layer, gain, ws, *, tm):
    n, d = x.shape
    dts = (F32, F32, BF16, F32)
    return pl.pallas_call(
        _proj_body,
        grid=(n // tm,),
        in_specs=[pl.BlockSpec((tm, d), lambda i: (i, 0)), pl.BlockSpec((1, d), lambda i: (0, 0))]
        + [_layer_spec(w, layer) for w in ws],
        out_specs=[pl.BlockSpec((tm, w.shape[2]), lambda i: (i, 0)) for w in ws],
        out_shape=[jax.ShapeDtypeStruct((n, w.shape[2]), dt) for w, dt in zip(ws, dts)],
        compiler_params=pltpu.CompilerParams(
            dimension_semantics=("parallel",), vmem_limit_bytes=VMEM_LIMIT),
        name="mix_proj",
    )(x, gain, *ws)


def _hgrn_steps(layer, q_ref, f_ref, i_ref, g_ref, lbl_ref, ng_ref, ctab_ref, gmat_ref, bd_ref,
                o_ref, st_scr):
    @pl.when(pl.program_id(1) == 0)
    def _():
        st_scr[...] = jnp.zeros_like(st_scr)

    logits = lbl_ref[...]
    e = jnp.exp(logits - jnp.max(logits, axis=0, keepdims=True))
    p = e / jnp.sum(e, axis=0, keepdims=True)
    lb = jnp.zeros_like(p[0:1])
    for r in range(1, layer + 1):
        lb = lb + p[r:r + 1]

    pair, eye = _level_masks(lane_stack=N_HEADS)
    hm = _head_masks(GROUP_W, HEAD_W, BF16)
    ctab = ctab_ref[...]
    gmat = gmat_ref[...]
    bd = bd_ref[...]
    ng = ng_ref[...]

    chunks = range(q_ref.shape[1] // CHUNK)
    rows_of = lambda c: pl.ds(c * CHUNK, CHUNK)
    qs, kk, cs = [], [], []
    for c in chunks:
        f = lb + (1.0 - lb) * _sigmoid_rel(f_ref[0, rows_of(c), :])
        qs.append(_silu(q_ref[0, rows_of(c), :]))
        kk.append(1.0 - f)
        cs.append(_select_rows(ctab, jnp.log(f)))
        yield 250

    scores = []
    for c in chunks:
        sc = eye * _dot_nt(_bf(qs[c]), _stack_heads(_bf(kk[c]), hm))
        for l, (fq, fk) in enumerate(_decay_factors(cs[c])):
            sc = sc + pair[l] * _dot_nt(_bf(qs[c] * fq), _stack_heads(_bf(kk[c] * fk), hm))
        scores.append(_bf(sc))
        yield 450

    qdec, b_last, o_intra, st_add = [], [], [], []
    for c in chunks:
        b = cs[c][0:CHUNK]
        rem = cs[c][CHUNK:2 * CHUNK]
        v = _bf(_silu(i_ref[0, rows_of(c), :]))
        o_intra.append(_dot(scores[c], _stack_heads(v, hm)))
        qdec.append(_bf(qs[c] * jnp.exp(b)))
        st_add.append(_dot_tn(v, _bf(kk[c] * jnp.exp(rem))) * bd)
        b_last.append(jnp.exp(b[CHUNK - 1:CHUNK]))
        yield 200

    st = st_scr[...]
    outs = []
    for c in chunks:
        outs.append(o_intra[c] + _dot_nt(qdec[c], _bf(st)))
        st = st * b_last[c] + st_add[c]
        yield 100
    st_scr[...] = st
    o_ref[0] = _bf(_head_rmsnorm(jnp.concatenate(outs, axis=0), gmat, ng) * _sigmoid(g_ref[0]))
    yield 150


def _weave(gens):
    spent = [0.0] * len(gens)
    live = list(range(len(gens)))
    while live:
        i = min(live, key=lambda j: spent[j])
        try:
            spent[i] += next(gens[i])
        except StopIteration:
            live.remove(i)


def _whole_spec(a):
    return pl.BlockSpec(a.shape, lambda b, t: (0,) * a.ndim)


def _col_spec(rows, width, c):
    return pl.BlockSpec((1, rows, width), lambda b, t: (b, t, c))


def _run_mixers(mixers, bsz, seq, rows, name):
    n_in = [len(m[1]) for m in mixers]
    n_scr = [len(m[3]) for m in mixers]

    def body(*refs):
        ins, outs, scr = refs[:sum(n_in)], refs[sum(n_in):sum(n_in) + len(mixers)], refs[sum(n_in) + len(mixers):]
        gens = []
        for i, m in enumerate(mixers):
            a, s = sum(n_in[:i]), sum(n_scr[:i])
            gens.append(m[0](*ins[a:a + n_in[i]], outs[i], *scr[s:s + n_scr[i]]))
        _weave(gens)

    out = pl.pallas_call(
        body,
        grid=(bsz, seq // rows),
        in_specs=[s for m in mixers for s in m[2]],
        out_specs=[pl.BlockSpec((1, rows, GROUP_W), lambda b, t: (b, t, 0)) for _ in mixers],
        out_shape=[jax.ShapeDtypeStruct((bsz, seq, GROUP_W), BF16) for _ in mixers],
        scratch_shapes=[s for m in mixers for s in m[3]],
        compiler_params=pltpu.CompilerParams(
            dimension_semantics=("parallel", "arbitrary"), vmem_limit_bytes=VMEM_LIMIT),
        name=name,
    )(*[a for m in mixers for a in m[1]])
    return out


def _hgrn_io(proj_a, lb_logits, norm_g, ctab, gmat, bd, layer, rows):
    small = [lb_logits, norm_g, ctab, gmat, bd]
    return (functools.partial(_hgrn_steps, layer), [proj_a] * 4 + small,
            [_col_spec(rows, GROUP_W, c) for c in range(4)] + [_whole_spec(a) for a in small],
            [pltpu.VMEM((GROUP_W, GROUP_W), F32)])


def _gdn_steps(q_ref, k_ref, v_ref, g_ref, a_ref, bl_ref, cw_ref, alog_ref, dtb_ref, ng_ref,
               ctab_ref, gmat_ref, bd_ref, o_ref, s_scr, tail_scr, ext_scr):
    @pl.when(pl.program_id(1) == 0)
    def _():
        s_scr[...] = jnp.zeros_like(s_scr)
        tail_scr[...] = jnp.zeros_like(tail_scr)

    pair, eye = _level_masks()
    hm = _head_masks(GROUP_W, HEAD_W, BF16)
    hmf = _head_masks(GROUP_W, HEAD_W, F32)
    ctab = ctab_ref[...]
    gmat = gmat_ref[...]
    bd = bd_ref[...]
    ng = ng_ref[...]
    neg_a = -jnp.exp(alog_ref[...])
    dtb = dtb_ref[...]
    raw = (q_ref, k_ref, v_ref)
    rows = q_ref.shape[1]

    for j in range(3):
        ext_scr[0:8, j * GROUP_W:(j + 1) * GROUP_W] = tail_scr[:, j * GROUP_W:(j + 1) * GROUP_W]
        ext_scr[8:8 + CHUNK, j * GROUP_W:(j + 1) * GROUP_W] = raw[j][0, 0:CHUNK, :]
        tail_scr[:, j * GROUP_W:(j + 1) * GROUP_W] = raw[j][0, rows - 8:rows, :]

    def conv(c, j):
        w = cw_ref[:, j * GROUP_W:(j + 1) * GROUP_W]
        acc = None
        for d in range(4):
            if c == 0:
                x = ext_scr[pl.ds(8 - d, CHUNK), j * GROUP_W:(j + 1) * GROUP_W]
            else:
                x = raw[j][0, pl.ds(c * CHUNK - d, CHUNK), :]
            term = x * w[3 - d:4 - d, :]
            acc = term if acc is None else acc + term
        return _silu(acc)

    nch = rows // CHUNK
    rr = lax.broadcasted_iota(jnp.int32, (N_HEADS * CHUNK, CHUNK), 0) & (CHUNK - 1)
    cc = lax.broadcasted_iota(jnp.int32, (N_HEADS * CHUNK, CHUNK), 1)
    tril = rr >= cc
    tril_f = tril.astype(F32)
    stril_f = (rr > cc).astype(F32)
    sub = lax.broadcasted_iota(jnp.int32, (1, GROUP_W), 1) & (HEAD_W - 1)
    lane_is = [(sub == i).astype(BF16) for i in range(6)]
    qc, kc, vc, qn, kn, cs, diff, prod, kb, beta = ({} for _ in range(10))
    qg, kdec, vb, kbg, g_last, attn, m_mat, t_inv = ({} for _ in range(8))
    o_lhs, o_add, s_lhs, s_add = {}, {}, {}, {}

    def phase1(grp):
        for c in grp:
            qc[c] = conv(c, 0)
            kc[c] = conv(c, 1)
            vc[c] = conv(c, 2)
            qn[c] = _headsum(qc[c] * qc[c], gmat)
            kn[c] = _headsum(kc[c] * kc[c], gmat)
            log_alpha = neg_a * _softplus(a_ref[0, pl.ds(c * CHUNK, CHUNK), :] + dtb)
            cs[c] = _select_rows(ctab[0:2 * CHUNK], log_alpha)
            yield
        for c in grp:
            qc[c] = qc[c] * lax.rsqrt(qn[c] + EPS) * (HEAD_W ** -0.5)
            kc[c] = kc[c] * lax.rsqrt(kn[c] + EPS)
            beta[c] = _sigmoid(bl_ref[0, pl.ds(c * CHUNK, CHUNK), :])
            kb[c] = kc[c] * beta[c]
            g1, g2, g3 = _split3(cs[c][0:CHUNK])
            lhs = g1 * lane_is[0] + g2 * lane_is[1] + g3 * lane_is[2] + (lane_is[3] + lane_is[4] + lane_is[5])
            rhs = (lane_is[0] + lane_is[1] + lane_is[2]) - g1 * lane_is[3] - g2 * lane_is[4] - g3 * lane_is[5]
            diff[c] = _dot_nt(_stack_heads(lhs, hm), rhs)
            prod[c] = _dot_nt(jnp.concatenate([_stack_heads(_bf(qc[c]), hm), _stack_heads(_bf(kb[c]), hm)],
                                              axis=0), _bf(kc[c]))
            yield
        for c in grp:
            gc = cs[c][0:CHUNK]
            rem = cs[c][CHUNK:2 * CHUNK]
            dec = jnp.exp(jnp.where(tril, diff[c], 0.0))
            attn_st = _bf(tril_f * dec * prod[c][0:N_HEADS * CHUNK])
            m_st = stril_f * dec * prod[c][N_HEADS * CHUNK:2 * N_HEADS * CHUNK]
            for h in range(N_HEADS):
                attn[c, h] = attn_st[h * CHUNK:(h + 1) * CHUNK]
                m_mat[c, h] = m_st[h * CHUNK:(h + 1) * CHUNK]
            qg[c] = qc[c] * jnp.exp(gc)
            kdec[c] = _bf(kc[c] * jnp.exp(rem))
            vb[c] = vc[c] * beta[c]
            kbg[c] = kb[c] * jnp.exp(gc)
            g_last[c] = jnp.exp(gc[CHUNK - 1:CHUNK])
            yield

    def phase2(grp):
        pairs = [(c, h) for c in grp for h in range(N_HEADS)]
        for p in pairs:
            t_inv[p] = eye - pair[N_LEVELS - 1] * m_mat[p]
        for l in range(N_LEVELS - 2, -1, -1):
            t_b = {p: _bf(t_inv[p]) for p in pairs}
            tl = {p: _dot(t_b[p], _bf(pair[l] * m_mat[p])) for p in pairs}
            yield
            for p in pairs:
                t_inv[p] = t_inv[p] - _dot(_bf(tl[p]), t_b[p])
            yield

    def phase3(grp):
        uw = {}
        for c in grp:
            rhs = jnp.concatenate([vb[c], kbg[c]], axis=1)
            acc = None
            for h in range(N_HEADS):
                lhs = jnp.concatenate(_split2(t_inv[c, h]), axis=1)
                rhs_h = _bf(rhs * jnp.concatenate([hmf[h], hmf[h]], axis=1))
                term = _dot(lhs, jnp.concatenate([rhs_h, rhs_h], axis=0))
                acc = term if acc is None else acc + term
            uw[c] = _bf(acc)
            yield
        for c in grp:
            u_b = uw[c][:, 0:GROUP_W]
            w_b = uw[c][:, GROUP_W:2 * GROUP_W]
            aw = None
            for h in range(N_HEADS):
                term = _dot(attn[c, h], jnp.concatenate([u_b * hm[h], w_b * hm[h]], axis=1))
                aw = term if aw is None else aw + term
            o_add[c] = aw[:, 0:GROUP_W]
            o_lhs[c] = _bf(qg[c] - aw[:, GROUP_W:2 * GROUP_W])
            ks = _dot_tn(kdec[c], uw[c])
            s_add[c] = ks[:, 0:GROUP_W] * bd
            s_lhs[c] = _bf(-ks[:, GROUP_W:2 * GROUP_W] * bd)
            yield

    state = [s_scr[...]]
    outs = []

    def phase4(grp):
        for c in grp:
            s_b = _bf(state[0])
            outs.append(o_add[c] + _dot(o_lhs[c], s_b))
            state[0] = state[0] * g_last[c] + _dot(s_lhs[c], s_b) + s_add[c]
            yield

    def chain(*gens):
        for gen in gens:
            yield from gen

    def run(primary, secondary=iter(()), per_step=1):
        for _ in primary:
            for _ in range(per_step):
                next(secondary, None)
            yield 400
        for _ in secondary:
            yield 200

    groups = [range(g, min(g + GDN_GROUP, nch)) for g in range(0, nch, GDN_GROUP)]
    yield from run(phase1(groups[0]))
    for gi, grp in enumerate(groups):
        fill = []
        if gi > 0:
            fill += [phase3(groups[gi - 1]), phase4(groups[gi - 1])]
        if gi + 1 < len(groups):
            fill.append(phase1(groups[gi + 1]))
        yield from run(phase2(grp), chain(*fill), per_step=2)
    yield from run(chain(phase3(groups[-1]), phase4(groups[-1])))
    s_scr[...] = state[0]
    o_ref[0] = _bf(_head_rmsnorm(jnp.concatenate(outs, axis=0), gmat, ng) * _silu(g_ref[0]))
    yield 200


def _gdn_io(proj_b, conv_w, a_log_rep, dt_bias_rep, norm_g, ctab, gmat, bd, rows):
    small = [conv_w, a_log_rep, dt_bias_rep, norm_g, ctab, gmat, bd]
    return (_gdn_steps, [proj_b] * 6 + small,
            [_col_spec(rows, GROUP_W, c) for c in range(6)] + [_whole_spec(a) for a in small],
            [pltpu.VMEM((GROUP_W, GROUP_W), F32),
             pltpu.VMEM((8, 3 * GROUP_W), F32),
             pltpu.VMEM((8 + CHUNK, 3 * GROUP_W), F32)])


def _sb_steps(q_ref, k_ref, v_ref, ng_ref, later_ref, gmat_ref, o_ref):
    blk, unit = SB_BLOCK, SB_UNIT
    assert blk == unit
    n_blk = q_ref.shape[1] // blk
    hm = _head_masks(GROUP_W, HEAD_W, BF16)
    later = later_ref[...]
    t_loc = lax.broadcasted_iota(jnp.int32, (N_HEADS * blk, unit), 0) & (blk - 1)
    s_loc = lax.broadcasted_iota(jnp.int32, (N_HEADS * blk, unit), 1)
    past0 = t_loc > s_loc

    def scores(qs, u, past):
        kblk = k_ref[0, pl.ds(u * unit, unit), :]
        z2 = _dot_nt(qs, kblk) * LOG2_E
        log_sig = jnp.minimum(z2, 0.0) - jnp.log2(1.0 + jnp.exp2(-jnp.abs(z2)))
        log_stay = log_sig - z2
        if past is not None:
            log_stay = jnp.where(past, log_stay, 0.0)
        sfx = _dot(_bf(log_stay), later)
        return log_sig + sfx, jnp.sum(log_stay, axis=-1, keepdims=True)

    def weights(arg, carry, past):
        wts = jnp.exp2(arg + carry)
        if past is not None:
            wts = jnp.where(past, wts, 0.0)
        wts = _bf(wts)
        return jnp.concatenate([wts[h * blk:(h + 1) * blk] for h in range(N_HEADS)], axis=1)

    def values(u, scale=None):
        vblk = v_ref[0, pl.ds(u * unit, unit), :]
        if scale is not None:
            vblk = vblk * scale
        return jnp.concatenate([vblk * hm[h] for h in range(N_HEADS)], axis=0)

    started = []
    for j in range(n_blk):
        u0 = pl.program_id(1) * n_blk + j
        q = q_ref[0, j * blk:(j + 1) * blk, :] * (HEAD_W ** -0.5)
        qs = jnp.concatenate([q * hm[h] for h in range(N_HEADS)], axis=0)
        has_prev = (u0 > 0).astype(F32)
        arg0, total0 = scores(qs, u0, past0)
        yield 900
        arg1, total1 = scores(qs, jnp.maximum(u0 - 1, 0), None)
        yield 900
        w0 = weights(arg0, 0.0, past0)
        w1 = weights(arg1, total0, None)
        yield 500
        v1 = values(jnp.maximum(u0 - 1, 0), jnp.broadcast_to(has_prev, (1, GROUP_W)).astype(BF16))
        acc = _dot(jnp.concatenate([w0, w1], axis=1), jnp.concatenate([values(u0), v1], axis=0))
        started.append((qs, u0, total0 + total1 * has_prev, acc))
        yield 500

    yield float("inf")

    for j, (qs, u0, carry, acc) in enumerate(started):
        def live(state):
            return jnp.logical_and(state[0] >= 0, state[1])

        def step(state, qs=qs):
            u, _, carry, acc = state
            arg, total = scores(qs, u, None)
            acc = acc + _dot(weights(arg, carry, None), values(u))
            carry = carry + total
            return u - 1, jnp.max(carry) > SB_DEAD, carry, acc

        _, _, carry, acc = lax.while_loop(live, step, (u0 - 2, jnp.max(carry) > SB_DEAD, carry, acc))
        o_ref[0, j * blk:(j + 1) * blk, :] = _bf(_head_rmsnorm(acc, gmat_ref[...], ng_ref[...]))


def _sb_io(proj_c, norm_g, later, gmat, rows):
    seq = proj_c.shape[1]
    small = [norm_g, later, gmat]
    return (_sb_steps, [proj_c] * 3 + small,
            [_col_spec(rows, GROUP_W, 0),
             pl.BlockSpec((1, seq, GROUP_W), lambda b, t: (b, 0, 1)),
             pl.BlockSpec((1, seq, GROUP_W), lambda b, t: (b, 0, 2))] + [_whole_spec(a) for a in small],
            [])


def _sb_table():
    j = np.arange(SB_UNIT)[:, None]
    s = np.arange(SB_UNIT)[None, :]
    return jnp.asarray((j > s).astype(np.float32), dtype=BF16)


def _ret_steps(q_ref, k_ref, v_ref, g_ref, cos_ref, sin_ref, decay_ref, zeta_ref, xi_ref, gc_ref,
               bd_ref, ng_ref, gmat_ref, o_ref, s_scr):
    @pl.when(pl.program_id(1) == 0)
    def _():
        s_scr[...] = jnp.zeros_like(s_scr)

    chunk = decay_ref.shape[0]
    decay = decay_ref[...]
    lane = lax.broadcasted_iota(jnp.int32, (1, 2 * HEAD_W), 1)
    hq = [(((lane % 64) // 16) == h).astype(BF16) for h in range(N_HEADS)]
    hv = _head_masks(GROUP_W, HEAD_W, BF16)
    gmat = gmat_ref[...]
    ng = ng_ref[...]
    bd = bd_ref[...]
    zeta = zeta_ref[...]
    xi = xi_ref[...]
    gamma_c = gc_ref[...]

    nch = q_ref.shape[1] // chunk
    qx, o_intra, s_add = [], [], []
    for c in range(nch):
        sl = pl.ds(c * chunk, chunk)
        cos = cos_ref[sl, :]
        sin = sin_ref[sl, :]
        q = q_ref[0, sl, :]
        k = k_ref[0, sl, :]
        qr = q * cos + pltpu.roll(q, 64, axis=1) * sin
        kr = (k * cos + pltpu.roll(k, 64, axis=1) * sin) * (D_DK ** -0.5)
        v = _bf(v_ref[0, sl, :])
        sc = _bf(_dot_nt(_bf(qr), _stack_heads(_bf(kr), hq)) * decay)
        o_intra.append(_dot(sc, _stack_heads(v, hv)))
        qx.append(_bf(qr * xi))
        s_add.append(_dot_tn(_bf(kr * zeta), v) * bd)
        yield 350

    s = s_scr[...]
    outs = []
    for c in range(nch):
        outs.append(o_intra[c] + _dot(qx[c], _bf(s)))
        s = s * gamma_c + s_add[c]
        yield 80
    s_scr[...] = s
    o_ref[0] = _bf(_head_rmsnorm(jnp.concatenate(outs, axis=0), gmat, ng) * _silu(g_ref[0]))
    yield 150


def _ret_io(proj_d, tables, norm_g, gmat, rows):
    cos, sin, decay, zeta, xi, gamma_c, bd = tables
    small = [decay, zeta, xi, gamma_c, bd, norm_g, gmat]
    return (_ret_steps, [proj_d] * 4 + [cos, sin] + small,
            [_col_spec(rows, 128, 0), _col_spec(rows, 128, 1), _col_spec(rows, GROUP_W, 1),
             _col_spec(rows, GROUP_W, 2),
             pl.BlockSpec((rows, 128), lambda b, t: (t, 0)), pl.BlockSpec((rows, 128), lambda b, t: (t, 0))]
            + [_whole_spec(a) for a in small],
            [pltpu.VMEM((2 * HEAD_W, GROUP_W), F32)])


def _mix_weights(w_in):
    w_in = _bf(w_in)
    a_w = w_in[..., 0:1024]
    b0 = 1024
    b_main = w_in[..., b0:b0 + 1024]
    b_a = jnp.repeat(w_in[..., b0 + 1024:b0 + 1028], HEAD_W, axis=-1)
    b_b = jnp.repeat(w_in[..., b0 + 1028:b0 + 1032], HEAD_W, axis=-1)
    c0 = b0 + 1032
    c_w = w_in[..., c0:c0 + 768]
    d0 = c0 + 768
    perm = _ret_perm()
    d_q = w_in[..., d0:d0 + 128][..., perm]
    d_k = w_in[..., d0 + 128:d0 + 256][..., perm]
    d_rest = w_in[..., d0 + 256:d0 + 768]
    b_w = jnp.concatenate([b_main, b_a, b_b], axis=-1)
    d_w = jnp.concatenate([d_q, d_k, d_rest], axis=-1)
    return a_w, b_w, c_w, d_w


def kernel(x, ffn1_norm, ffn1_w_in, ffn1_w_out, mix_norm, mix_w_in, mix_w_out, ffn2_norm, ffn2_w_in,
           ffn2_w_out, hgrn_lb_logits, hgrn_out_norm, gdn_conv_w, gdn_a_log, gdn_dt_bias, gdn_out_norm,
           sb_out_norm, ret_out_norm, final_norm):
    bsz, seq, d = x.shape
    depth = ffn1_norm.shape[0]
    n = bsz * seq
    assert ffn1_w_out.shape[1] % FFN_COLS == 0 and n % FFN_ROWS == 0 and n % PROJ_ROWS == 0
    assert seq % MIXER_ROWS == 0 and MIXER_ROWS % SB_BLOCK == 0 and MIXER_ROWS % RET_CHUNK == 0
    rows = MIXER_ROWS

    ctab = _cumsum_table()
    gmat = _head_block_matrix(GROUP_W, HEAD_W, GROUP_W, HEAD_W, BF16)
    bd = _head_block_matrix(GROUP_W, HEAD_W, GROUP_W, HEAD_W, F32)
    u2 = _sb_table()
    ret_tables = _retention_tables(seq, RET_CHUNK)
    row = lambda a: a.reshape(1, -1).astype(F32)
    rep = lambda a: jnp.repeat(a.astype(F32), HEAD_W).reshape(1, GROUP_W)

    w1_in, w1_out, w2_in, w2_out, w_mix_out = (_bf(w) for w in (ffn1_w_in, ffn1_w_out, ffn2_w_in, ffn2_w_out,
                                                                  mix_w_out))
    w_mix_in = _mix_weights(mix_w_in)

    x = x.reshape(n, d)
    for l in range(depth):
        x = _ffn(x, l, row(ffn1_norm[l]), w1_in, w1_out, tm=FFN_ROWS)
        pa, pb, pc, pd = _mix_proj(x, l, row(mix_norm[l]), w_mix_in, tm=PROJ_ROWS)
        shp = lambda a: a.reshape(bsz, seq, a.shape[-1])
        ya, yb, yc, yd = _run_mixers(
            [_hgrn_io(shp(pa), hgrn_lb_logits.astype(F32), row(hgrn_out_norm[l]), ctab, gmat, bd, l, rows),
             _gdn_io(shp(pb), gdn_conv_w[l].astype(F32), rep(gdn_a_log[l]), rep(gdn_dt_bias[l]),
                     row(gdn_out_norm[l]), ctab, gmat, bd, rows),
             _sb_io(shp(pc), row(sb_out_norm[l]), u2, gmat, rows),
             _ret_io(shp(pd), ret_tables, row(ret_out_norm[l]), gmat, rows)],
            bsz, seq, rows, "mixers")
        ys = [y.reshape(n, GROUP_W) for y in (ya, yb, yc, yd)]
        x = _ffn(x, l, row(ffn2_norm[l]), w2_in, w2_out, tm=FFN_ROWS, mix=(ys, w_mix_out),
                 final_gain=row(final_norm) if l == depth - 1 else None)
    return x.reshape(bsz, seq, d)
```

```python
import functools

import numpy as np
import jax
import jax.numpy as jnp
from jax import lax
from jax.experimental import pallas as pl
from jax.experimental.pallas import tpu as pltpu

F32 = jnp.float32
BF16 = jnp.bfloat16

EPS = 1e-6
CHUNK = 64
N_HEADS = 4
HEAD_W = 64
GROUP_W = N_HEADS * HEAD_W
D_DK = 32
ROPE_BASE = 10000.0
N_LEVELS = 6
VMEM_LIMIT = 56 * 1024 * 1024

FFN_ROWS = 1024
PROJ_ROWS = 512
MIXER_ROWS = 512
GDN_GROUP = 8
FFN_COLS = 256
RET_CHUNK = 128
SB_BLOCK = 256
SB_UNIT = 256
SB_DEAD = -151.0
LOG2_E = 1.4426950408889634


def _bf(x):
    return x.astype(BF16)


def _dot(a, b):
    return jnp.dot(a, b, preferred_element_type=F32)


def _dot_nt(a, b):
    return lax.dot_general(a, b, (((1,), (1,)), ((), ())), preferred_element_type=F32)


def _dot_tn(a, b):
    return lax.dot_general(a, b, (((0,), (0,)), ((), ())), preferred_element_type=F32)


def _split2(x):
    hi = _bf(x)
    lo = _bf(x - hi.astype(F32))
    return hi, lo


def _split3(x):
    h1 = _bf(x)
    r = x - h1.astype(F32)
    h2 = _bf(r)
    h3 = _bf(r - h2.astype(F32))
    return h1, h2, h3


def _select_rows(w3, x):
    return _dot(w3, jnp.concatenate(_split3(x), axis=0))


def _headsum(x, gmat):
    hi, lo = _split2(x)
    return _dot(hi, gmat) + _dot(lo, gmat)


def _head_rmsnorm(o, gmat, gain):
    ms = _headsum(o * o, gmat) * (1.0 / HEAD_W)
    return o * lax.rsqrt(ms + EPS) * gain


def _sigmoid(x):
    return 0.5 * jnp.tanh(0.5 * x) + 0.5


def _sigmoid_rel(x):
    return jnp.exp(jnp.minimum(x, 0.0) - jnp.log(1.0 + jnp.exp(-jnp.abs(x))))


def _silu(x):
    return x * _sigmoid(x)


def _softplus(x):
    return jnp.maximum(x, 0.0) + jnp.log(1.0 + jnp.exp(-jnp.abs(x)))


def _head_masks(width, lanes_per_head, dtype):
    lane = lax.broadcasted_iota(jnp.int32, (1, width), 1)
    return [((lane // lanes_per_head) == h).astype(dtype) for h in range(N_HEADS)]


def _stack_heads(x, masks):
    return jnp.concatenate([x * m for m in masks], axis=0)


def _level_masks(lane_stack=1):
    r = lax.broadcasted_iota(jnp.int32, (CHUNK, lane_stack * CHUNK), 0)
    c = lax.broadcasted_iota(jnp.int32, (CHUNK, lane_stack * CHUNK), 1) & (CHUNK - 1)
    pair = []
    for l in range(N_LEVELS):
        n = CHUNK >> l
        m = n // 2
        sh = N_LEVELS - l
        same = (r >> sh) == (c >> sh)
        pair.append((same & ((r & (n - 1)) >= m) & ((c & (n - 1)) < m)).astype(F32))
    eye = (r == c).astype(F32)
    return pair, eye


def _decay_factors(cs):
    return [(jnp.exp(cs[(2 + l) * CHUNK:(3 + l) * CHUNK]),
             jnp.exp(cs[(2 + N_LEVELS + l) * CHUNK:(3 + N_LEVELS + l) * CHUNK])) for l in range(N_LEVELS)]


def _cumsum_table():
    t = np.arange(CHUNK)
    tri = (t[None, :] <= t[:, None]).astype(np.float32)
    up = (t[None, :] > t[:, None]).astype(np.float32)
    q_blocks, k_blocks = [], []
    for l in range(N_LEVELS):
        n = CHUNK >> l
        m = n // 2
        anchor = (t // n) * n + m - 1
        upper = ((t % n) >= m)[:, None]
        q_blocks.append(np.where(upper, tri - tri[anchor], 0.0))
        k_blocks.append(np.where(~upper, tri[anchor] - tri, 0.0))
    tab = np.concatenate([tri, up] + q_blocks + k_blocks, axis=0)
    assert set(np.unique(tab)) <= {0.0, 1.0}
    return jnp.asarray(np.concatenate([tab, tab, tab], axis=1), dtype=BF16)


def _head_block_matrix(rows, rows_per_head, cols, cols_per_head, dtype):
    r = np.arange(rows)[:, None] // rows_per_head
    c = np.arange(cols)[None, :] // cols_per_head
    return jnp.asarray((r == c).astype(np.float32), dtype=dtype)


def _ret_lane_head(p):
    return (p % 64) // 16


def _retention_tables(seq, chunk):
    p = np.arange(128)
    half = D_DK // 2
    inv_freq = ROPE_BASE ** (-(p % 16).astype(np.float64) / half)
    ang = np.arange(seq, dtype=np.float64)[:, None] * inv_freq[None, :]
    cos = np.cos(ang)
    sin = np.sin(ang) * np.where(p < 64, -1.0, 1.0)[None, :]
    log_gamma = np.log(1.0 - 2.0 ** (-5.0 - np.arange(N_HEADS, dtype=np.float64)))
    c = np.arange(chunk, dtype=np.float64)
    rel = c[:, None] - c[None, :]
    decay = np.where(rel[None] >= 0, np.exp(rel[None] * log_gamma[:, None, None]), 0.0)
    decay = decay.transpose(1, 0, 2).reshape(chunk, N_HEADS * chunk)
    lane_h = _ret_lane_head(p)
    zeta = np.exp((chunk - 1 - c)[:, None] * log_gamma[lane_h][None, :])
    xi = np.exp((c + 1.0)[:, None] * log_gamma[lane_h][None, :])
    gamma_c = np.exp(chunk * log_gamma)[np.arange(GROUP_W) // HEAD_W][None, :]
    bd = (lane_h[:, None] == (np.arange(GROUP_W) // HEAD_W)[None, :]).astype(np.float32)
    f = lambda a: jnp.asarray(a, dtype=F32)
    return f(cos), f(sin), f(decay), f(zeta), f(xi), f(gamma_c), f(bd)


def _ret_perm():
    p = np.arange(128)
    h = _ret_lane_head(p)
    return h * D_DK + (p % 16) + np.where(p >= 64, 16, 0)


def _rms_rows(x, gain):
    return x * lax.rsqrt(jnp.mean(x * x, axis=-1, keepdims=True) + EPS) * gain


def _ffn_body(has_mix, has_final, *refs):
    refs = list(refs)
    x_ref = refs.pop(0)
    y_refs = [refs.pop(0) for _ in range(4)] if has_mix else []
    wmix_ref = refs.pop(0) if has_mix else None
    g_ref, win_ref, wout_ref = refs.pop(0), refs.pop(0), refs.pop(0)
    gf_ref = refs.pop(0) if has_final else None
    o_ref = refs.pop(0)

    x = x_ref[...]
    for m, y_ref in enumerate(y_refs):
        x = x + _dot(y_ref[...], wmix_ref[m * GROUP_W:(m + 1) * GROUP_W, :])
    h = _bf(_rms_rows(x, g_ref[...]))

    d_ff = wout_ref.shape[0]
    steps = d_ff // FFN_COLS
    acc = None
    act = None
    for c in range(steps + 1):
        if c < steps:
            gate = _dot(h, win_ref[:, c * FFN_COLS:(c + 1) * FFN_COLS])
            up = _dot(h, win_ref[:, d_ff + c * FFN_COLS:d_ff + (c + 1) * FFN_COLS])
        if act is not None:
            down = _dot(act, wout_ref[(c - 1) * FFN_COLS:c * FFN_COLS, :])
            acc = down if acc is None else acc + down
        if c < steps:
            act = _bf(_silu(gate) * up)
    y = x + 0.5 * acc
    if has_final:
        y = _rms_rows(y, gf_ref[...])
    o_ref[...] = y


def _layer_spec(a, layer):
    return pl.BlockSpec((None,) + a.shape[1:], lambda i: (layer, 0, 0), pipeline_mode=pl.Buffered(1))


def _ffn(x, layer, gain, w_in, w_out, *, tm, mix=None, final_gain=None):
    n, d = x.shape
    rows = lambda w: pl.BlockSpec((tm, w), lambda i: (i, 0))
    whole = lambda a: _layer_spec(a, layer) if a.ndim == 3 else pl.BlockSpec(a.shape, lambda i: (0, 0))
    args, specs = [x], [rows(d)]
    if mix is not None:
        ys, w_mix = mix
        args += list(ys) + [w_mix]
        specs += [rows(GROUP_W) for _ in ys] + [whole(w_mix)]
    args += [gain, w_in, w_out]
    specs += [whole(gain), whole(w_in), whole(w_out)]
    if final_gain is not None:
        args.append(final_gain)
        specs.append(whole(final_gain))
    return pl.pallas_call(
        functools.partial(_ffn_body, mix is not None, final_gain is not None),
        grid=(n // tm,),
        in_specs=specs,
        out_specs=rows(d),
        out_shape=jax.ShapeDtypeStruct((n, d), F32),
        compiler_params=pltpu.CompilerParams(
            dimension_semantics=("parallel",), vmem_limit_bytes=VMEM_LIMIT),
        name="ffn",
    )(*args)


def _proj_body(x_ref, g_ref, wa_ref, wb_ref, wc_ref, wd_ref, oa_ref, ob_ref, oc_ref, od_ref):
    h = _bf(_rms_rows(x_ref[...], g_ref[...]))
    oa_ref[...] = _dot(h, wa_ref[...])
    ob_ref[...] = _dot(h, wb_ref[...])
    oc_ref[...] = _bf(_dot(h, wc_ref[...]))
    od_ref[...] = _dot(h, wd_ref[...])


def _mix_proj(x, layer, gain, ws, *, tm):
    n, d = x.shape
    dts = (F32, F32, BF16, F32)
    return pl.pallas_call(
        _proj_body,
        grid=(n // tm,),
        in_specs=[pl.BlockSpec((tm, d), lambda i: (i, 0)), pl.BlockSpec((1, d), lambda i: (0, 0))]
        + [_layer_spec(w, layer) for w in ws],
        out_specs=[pl.BlockSpec((tm, w.shape[2]), lambda i: (i, 0)) for w in ws],
        out_shape=[jax.ShapeDtypeStruct((n, w.shape[2]), dt) for w, dt in zip(ws, dts)],
        compiler_params=pltpu.CompilerParams(
            dimension_semantics=("parallel",), vmem_limit_bytes=VMEM_LIMIT),
        name="mix_proj",
    )(x, gain, *ws)


def _hgrn_steps(layer, q_ref, f_ref, i_ref, g_ref, lbl_ref, ng_ref, ctab_ref, gmat_ref, bd_ref,
                o_ref, st_scr):
    @pl.when(pl.program_id(1) == 0)
    def _():
        st_scr[...] = jnp.zeros_like(st_scr)

    logits = lbl_ref[...]
    e = jnp.exp(logits - jnp.max(logits, axis=0, keepdims=True))
    p = e / jnp.sum(e, axis=0, keepdims=True)
    lb = jnp.zeros_like(p[0:1])
    for r in range(1, layer + 1):
        lb = lb + p[r:r + 1]

    pair, eye = _level_masks(lane_stack=N_HEADS)
    hm = _head_masks(GROUP_W, HEAD_W, BF16)
    ctab = ctab_ref[...]
    gmat = gmat_ref[...]
    bd = bd_ref[...]
    ng = ng_ref[...]

    chunks = range(q_ref.shape[1] // CHUNK)
    rows_of = lambda c: pl.ds(c * CHUNK, CHUNK)
    qs, kk, cs = [], [], []
    for c in chunks:
        f = lb + (1.0 - lb) * _sigmoid_rel(f_ref[0, rows_of(c), :])
        qs.append(_silu(q_ref[0, rows_of(c), :]))
        kk.append(1.0 - f)
        cs.append(_select_rows(ctab, jnp.log(f)))
        yield 250

    scores = []
    for c in chunks:
        sc = eye * _dot_nt(_bf(qs[c]), _stack_heads(_bf(kk[c]), hm))
        for l, (fq, fk) in enumerate(_decay_factors(cs[c])):
            sc = sc + pair[l] * _dot_nt(_bf(qs[c] * fq), _stack_heads(_bf(kk[c] * fk), hm))
        scores.append(_bf(sc))
        yield 450

    qdec, b_last, o_intra, st_add = [], [], [], []
    for c in chunks:
        b = cs[c][0:CHUNK]
        rem = cs[c][CHUNK:2 * CHUNK]
        v = _bf(_silu(i_ref[0, rows_of(c), :]))
        o_intra.append(_dot(scores[c], _stack_heads(v, hm)))
        qdec.append(_bf(qs[c] * jnp.exp(b)))
        st_add.append(_dot_tn(v, _bf(kk[c] * jnp.exp(rem))) * bd)
        b_last.append(jnp.exp(b[CHUNK - 1:CHUNK]))
        yield 200

    st = st_scr[...]
    outs = []
    for c in chunks:
        outs.append(o_intra[c] + _dot_nt(qdec[c], _bf(st)))
        st = st * b_last[c] + st_add[c]
        yield 100
    st_scr[...] = st
    o_ref[0] = _bf(_head_rmsnorm(jnp.concatenate(outs, axis=0), gmat, ng) * _sigmoid(g_ref[0]))
    yield 150


def _weave(gens):
    spent = [0.0] * len(gens)
    live = list(range(len(gens)))
    while live:
        i = min(live, key=lambda j: spent[j])
        try:
            spent[i] += next(gens[i])
        except StopIteration:
            live.remove(i)


def _whole_spec(a):
    return pl.BlockSpec(a.shape, lambda b, t: (0,) * a.ndim)


def _col_spec(rows, width, c):
    return pl.BlockSpec((1, rows, width), lambda b, t: (b, t, c))


def _run_mixers(mixers, bsz, seq, rows, name):
    n_in = [len(m[1]) for m in mixers]
    n_scr = [len(m[3]) for m in mixers]

    def body(*refs):
        ins, outs, scr = refs[:sum(n_in)], refs[sum(n_in):sum(n_in) + len(mixers)], refs[sum(n_in) + len(mixers):]
        gens = []
        for i, m in enumerate(mixers):
            a, s = sum(n_in[:i]), sum(n_scr[:i])
            gens.append(m[0](*ins[a:a + n_in[i]], outs[i], *scr[s:s + n_scr[i]]))
        _weave(gens)

    out = pl.pallas_call(
        body,
        grid=(bsz, seq // rows),
        in_specs=[s for m in mixers for s in m[2]],
        out_specs=[pl.BlockSpec((1, rows, GROUP_W), lambda b, t: (b, t, 0)) for _ in mixers],
        out_shape=[jax.ShapeDtypeStruct((bsz, seq, GROUP_W), BF16) for _ in mixers],
        scratch_shapes=[s for m in mixers for s in m[3]],
        compiler_params=pltpu.CompilerParams(
            dimension_semantics=("parallel", "arbitrary"), vmem_limit_bytes=VMEM_LIMIT),
        name=name,
    )(*[a for m in mixers for a in m[1]])
    return out


def _hgrn_io(proj_a, lb_logits, norm_g, ctab, gmat, bd, layer, rows):
    small = [lb_logits, norm_g, ctab, gmat, bd]
    return (functools.partial(_hgrn_steps, layer), [proj_a] * 4 + small,
            [_col_spec(rows, GROUP_W, c) for c in range(4)] + [_whole_spec(a) for a in small],
            [pltpu.VMEM((GROUP_W, GROUP_W), F32)])


def _gdn_steps(q_ref, k_ref, v_ref, g_ref, a_ref, bl_ref, cw_ref, alog_ref, dtb_ref, ng_ref,
               ctab_ref, gmat_ref, bd_ref, o_ref, s_scr, tail_scr, ext_scr):
    @pl.when(pl.program_id(1) == 0)
    def _():
        s_scr[...] = jnp.zeros_like(s_scr)
        tail_scr[...] = jnp.zeros_like(tail_scr)

    pair, eye = _level_masks()
    hm = _head_masks(GROUP_W, HEAD_W, BF16)
    hmf = _head_masks(GROUP_W, HEAD_W, F32)
    ctab = ctab_ref[...]
    gmat = gmat_ref[...]
    bd = bd_ref[...]
    ng = ng_ref[...]
    neg_a = -jnp.exp(alog_ref[...])
    dtb = dtb_ref[...]
    raw = (q_ref, k_ref, v_ref)
    rows = q_ref.shape[1]

    for j in range(3):
        ext_scr[0:8, j * GROUP_W:(j + 1) * GROUP_W] = tail_scr[:, j * GROUP_W:(j + 1) * GROUP_W]
        ext_scr[8:8 + CHUNK, j * GROUP_W:(j + 1) * GROUP_W] = raw[j][0, 0:CHUNK, :]
        tail_scr[:, j * GROUP_W:(j + 1) * GROUP_W] = raw[j][0, rows - 8:rows, :]

    def conv(c, j):
        w = cw_ref[:, j * GROUP_W:(j + 1) * GROUP_W]
        acc = None
        for d in range(4):
            if c == 0:
                x = ext_scr[pl.ds(8 - d, CHUNK), j * GROUP_W:(j + 1) * GROUP_W]
            else:
                x = raw[j][0, pl.ds(c * CHUNK - d, CHUNK), :]
            term = x * w[3 - d:4 - d, :]
            acc = term if acc is None else acc + term
        return _silu(acc)

    nch = rows // CHUNK
    rr = lax.broadcasted_iota(jnp.int32, (N_HEADS * CHUNK, CHUNK), 0) & (CHUNK - 1)
    cc = lax.broadcasted_iota(jnp.int32, (N_HEADS * CHUNK, CHUNK), 1)
    tril = rr >= cc
    tril_f = tril.astype(F32)
    stril_f = (rr > cc).astype(F32)
    sub = lax.broadcasted_iota(jnp.int32, (1, GROUP_W), 1) & (HEAD_W - 1)
    lane_is = [(sub == i).astype(BF16) for i in range(6)]
    qc, kc, vc, qn, kn, cs, diff, prod, kb, beta = ({} for _ in range(10))
    qg, kdec, vb, kbg, g_last, attn, m_mat, t_inv = ({} for _ in range(8))
    o_lhs, o_add, s_lhs, s_add = {}, {}, {}, {}

    def phase1(grp):
        for c in grp:
            qc[c] = conv(c, 0)
            kc[c] = conv(c, 1)
            vc[c] = conv(c, 2)
            qn[c] = _headsum(qc[c] * qc[c], gmat)
            kn[c] = _headsum(kc[c] * kc[c], gmat)
            log_alpha = neg_a * _softplus(a_ref[0, pl.ds(c * CHUNK, CHUNK), :] + dtb)
            cs[c] = _select_rows(ctab[0:2 * CHUNK], log_alpha)
            yield
        for c in grp:
            qc[c] = qc[c] * lax.rsqrt(qn[c] + EPS) * (HEAD_W ** -0.5)
            kc[c] = kc[c] * lax.rsqrt(kn[c] + EPS)
            beta[c] = _sigmoid(bl_ref[0, pl.ds(c * CHUNK, CHUNK), :])
            kb[c] = kc[c] * beta[c]
            g1, g2, g3 = _split3(cs[c][0:CHUNK])
            lhs = g1 * lane_is[0] + g2 * lane_is[1] + g3 * lane_is[2] + (lane_is[3] + lane_is[4] + lane_is[5])
            rhs = (lane_is[0] + lane_is[1] + lane_is[2]) - g1 * lane_is[3] - g2 * lane_is[4] - g3 * lane_is[5]
            diff[c] = _dot_nt(_stack_heads(lhs, hm), rhs)
            prod[c] = _dot_nt(jnp.concatenate([_stack_heads(_bf(qc[c]), hm), _stack_heads(_bf(kb[c]), hm)],
                                              axis=0), _bf(kc[c]))
            yield
        for c in grp:
            gc = cs[c][0:CHUNK]
            rem = cs[c][CHUNK:2 * CHUNK]
            dec = jnp.exp(jnp.where(tril, diff[c], 0.0))
            attn_st = _bf(tril_f * dec * prod[c][0:N_HEADS * CHUNK])
            m_st = stril_f * dec * prod[c][N_HEADS * CHUNK:2 * N_HEADS * CHUNK]
            for h in range(N_HEADS):
                attn[c, h] = attn_st[h * CHUNK:(h + 1) * CHUNK]
                m_mat[c, h] = m_st[h * CHUNK:(h + 1) * CHUNK]
            qg[c] = qc[c] * jnp.exp(gc)
            kdec[c] = _bf(kc[c] * jnp.exp(rem))
            vb[c] = vc[c] * beta[c]
            kbg[c] = kb[c] * jnp.exp(gc)
            g_last[c] = jnp.exp(gc[CHUNK - 1:CHUNK])
            yield

    def phase2(grp):
        pairs = [(c, h) for c in grp for h in range(N_HEADS)]
        for p in pairs:
            t_inv[p] = eye - pair[N_LEVELS - 1] * m_mat[p]
        for l in range(N_LEVELS - 2, -1, -1):
            t_b = {p: _bf(t_inv[p]) for p in pairs}
            tl = {p: _dot(t_b[p], _bf(pair[l] * m_mat[p])) for p in pairs}
            yield
            for p in pairs:
                t_inv[p] = t_inv[p] - _dot(_bf(tl[p]), t_b[p])
            yield

    def phase3(grp):
        uw = {}
        for c in grp:
            rhs = jnp.concatenate([vb[c], kbg[c]], axis=1)
            acc = None
            for h in range(N_HEADS):
                rhs_h = _bf(rhs * jnp.concatenate([hmf[h], hmf[h]], axis=1))
                term = _dot(_bf(t_inv[c, h]), rhs_h)
                acc = term if acc is None else acc + term
            uw[c] = _bf(acc)
            yield
        for c in grp:
            u_b = uw[c][:, 0:GROUP_W]
            w_b = uw[c][:, GROUP_W:2 * GROUP_W]
            aw = None
            for h in range(N_HEADS):
                term = _dot(attn[c, h], jnp.concatenate([u_b * hm[h], w_b * hm[h]], axis=1))
                aw = term if aw is None else aw + term
            o_add[c] = aw[:, 0:GROUP_W]
            o_lhs[c] = _bf(qg[c] - aw[:, GROUP_W:2 * GROUP_W])
            ks = _dot_tn(kdec[c], uw[c])
            s_add[c] = ks[:, 0:GROUP_W] * bd
            s_lhs[c] = _bf(-ks[:, GROUP_W:2 * GROUP_W] * bd)
            yield

    state = [s_scr[...]]
    outs = []

    def phase4(grp):
        for c in grp:
            s_b = _bf(state[0])
            outs.append(o_add[c] + _dot(o_lhs[c], s_b))
            state[0] = state[0] * g_last[c] + _dot(s_lhs[c], s_b) + s_add[c]
            yield

    def chain(*gens):
        for gen in gens:
            yield from gen

    def run(primary, secondary=iter(()), per_step=1):
        for _ in primary:
            for _ in range(per_step):
                next(secondary, None)
            yield 400
        for _ in secondary:
            yield 200

    groups = [range(g, min(g + GDN_GROUP, nch)) for g in range(0, nch, GDN_GROUP)]
    yield from run(phase1(groups[0]))
    for gi, grp in enumerate(groups):
        fill = []
        if gi > 0:
            fill += [phase3(groups[gi - 1]), phase4(groups[gi - 1])]
        if gi + 1 < len(groups):
            fill.append(phase1(groups[gi + 1]))
        yield from run(phase2(grp), chain(*fill), per_step=2)
    yield from run(chain(phase3(groups[-1]), phase4(groups[-1])))
    s_scr[...] = state[0]
    o_ref[0] = _bf(_head_rmsnorm(jnp.concatenate(outs, axis=0), gmat, ng) * _silu(g_ref[0]))
    yield 200


def _gdn_io(proj_b, conv_w, a_log_rep, dt_bias_rep, norm_g, ctab, gmat, bd, rows):
    small = [conv_w, a_log_rep, dt_bias_rep, norm_g, ctab, gmat, bd]
    return (_gdn_steps, [proj_b] * 6 + small,
            [_col_spec(rows, GROUP_W, c) for c in range(6)] + [_whole_spec(a) for a in small],
            [pltpu.VMEM((GROUP_W, GROUP_W), F32),
             pltpu.VMEM((8, 3 * GROUP_W), F32),
             pltpu.VMEM((8 + CHUNK, 3 * GROUP_W), F32)])


def _sb_steps(q_ref, k_ref, v_ref, ng_ref, later_ref, gmat_ref, o_ref):
    blk, unit = SB_BLOCK, SB_UNIT
    assert blk == unit
    n_blk = q_ref.shape[1] // blk
    hm = _head_masks(GROUP_W, HEAD_W, BF16)
    later = later_ref[...]
    t_loc = lax.broadcasted_iota(jnp.int32, (N_HEADS * blk, unit), 0) & (blk - 1)
    s_loc = lax.broadcasted_iota(jnp.int32, (N_HEADS * blk, unit), 1)
    past0 = t_loc > s_loc

    def scores(qs, u, past):
        kblk = k_ref[0, pl.ds(u * unit, unit), :]
        z2 = _dot_nt(qs, kblk) * LOG2_E
        log_sig = jnp.minimum(z2, 0.0) - jnp.log2(1.0 + jnp.exp2(-jnp.abs(z2)))
        log_stay = log_sig - z2
        if past is not None:
            log_stay = jnp.where(past, log_stay, 0.0)
        sfx = _dot(_bf(log_stay), later)
        return log_sig + sfx, jnp.sum(log_stay, axis=-1, keepdims=True)

    def weights(arg, carry, past):
        wts = jnp.exp2(arg + carry)
        if past is not None:
            wts = jnp.where(past, wts, 0.0)
        wts = _bf(wts)
        return jnp.concatenate([wts[h * blk:(h + 1) * blk] for h in range(N_HEADS)], axis=1)

    def values(u, scale=None):
        vblk = v_ref[0, pl.ds(u * unit, unit), :]
        if scale is not None:
            vblk = vblk * scale
        return jnp.concatenate([vblk * hm[h] for h in range(N_HEADS)], axis=0)

    started = []
    for j in range(n_blk):
        u0 = pl.program_id(1) * n_blk + j
        q = q_ref[0, j * blk:(j + 1) * blk, :] * (HEAD_W ** -0.5)
        qs = jnp.concatenate([q * hm[h] for h in range(N_HEADS)], axis=0)
        has_prev = (u0 > 0).astype(F32)
        arg0, total0 = scores(qs, u0, past0)
        yield 900
        arg1, total1 = scores(qs, jnp.maximum(u0 - 1, 0), None)
        yield 900
        w0 = weights(arg0, 0.0, past0)
        w1 = weights(arg1, total0, None)
        yield 500
        v1 = values(jnp.maximum(u0 - 1, 0), jnp.broadcast_to(has_prev, (1, GROUP_W)).astype(BF16))
        acc = _dot(jnp.concatenate([w0, w1], axis=1), jnp.concatenate([values(u0), v1], axis=0))
        started.append((qs, u0, total0 + total1 * has_prev, acc))
        yield 500

    yield float("inf")

    for j, (qs, u0, carry, acc) in enumerate(started):
        def live(state):
            return jnp.logical_and(state[0] >= 0, state[1])

        def step(state, qs=qs):
            u, _, carry, acc = state
            arg, total = scores(qs, u, None)
            acc = acc + _dot(weights(arg, carry, None), values(u))
            carry = carry + total
            return u - 1, jnp.max(carry) > SB_DEAD, carry, acc

        _, _, carry, acc = lax.while_loop(live, step, (u0 - 2, jnp.max(carry) > SB_DEAD, carry, acc))
        o_ref[0, j * blk:(j + 1) * blk, :] = _bf(_head_rmsnorm(acc, gmat_ref[...], ng_ref[...]))


def _sb_io(proj_c, norm_g, later, gmat, rows):
    seq = proj_c.shape[1]
    small = [norm_g, later, gmat]
    return (_sb_steps, [proj_c] * 3 + small,
            [_col_spec(rows, GROUP_W, 0),
             pl.BlockSpec((1, seq, GROUP_W), lambda b, t: (b, 0, 1)),
             pl.BlockSpec((1, seq, GROUP_W), lambda b, t: (b, 0, 2))] + [_whole_spec(a) for a in small],
            [])


def _sb_table():
    j = np.arange(SB_UNIT)[:, None]
    s = np.arange(SB_UNIT)[None, :]
    return jnp.asarray((j > s).astype(np.float32), dtype=BF16)


def _ret_steps(q_ref, k_ref, v_ref, g_ref, cos_ref, sin_ref, decay_ref, zeta_ref, xi_ref, gc_ref,
               bd_ref, ng_ref, gmat_ref, o_ref, s_scr):
    @pl.when(pl.program_id(1) == 0)
    def _():
        s_scr[...] = jnp.zeros_like(s_scr)

    chunk = decay_ref.shape[0]
    decay = decay_ref[...]
    lane = lax.broadcasted_iota(jnp.int32, (1, 2 * HEAD_W), 1)
    hq = [(((lane % 64) // 16) == h).astype(BF16) for h in range(N_HEADS)]
    hv = _head_masks(GROUP_W, HEAD_W, BF16)
    gmat = gmat_ref[...]
    ng = ng_ref[...]
    bd = bd_ref[...]
    zeta = zeta_ref[...]
    xi = xi_ref[...]
    gamma_c = gc_ref[...]

    nch = q_ref.shape[1] // chunk
    qx, o_intra, s_add = [], [], []
    for c in range(nch):
        sl = pl.ds(c * chunk, chunk)
        cos = cos_ref[sl, :]
        sin = sin_ref[sl, :]
        q = q_ref[0, sl, :]
        k = k_ref[0, sl, :]
        qr = q * cos + pltpu.roll(q, 64, axis=1) * sin
        kr = (k * cos + pltpu.roll(k, 64, axis=1) * sin) * (D_DK ** -0.5)
        v = _bf(v_ref[0, sl, :])
        sc = _bf(_dot_nt(_bf(qr), _stack_heads(_bf(kr), hq)) * decay)
        o_intra.append(_dot(sc, _stack_heads(v, hv)))
        qx.append(_bf(qr * xi))
        s_add.append(_dot_tn(_bf(kr * zeta), v) * bd)
        yield 350

    s = s_scr[...]
    outs = []
    for c in range(nch):
        outs.append(o_intra[c] + _dot(qx[c], _bf(s)))
        s = s * gamma_c + s_add[c]
        yield 80
    s_scr[...] = s
    o_ref[0] = _bf(_head_rmsnorm(jnp.concatenate(outs, axis=0), gmat, ng) * _silu(g_ref[0]))
    yield 150


def _ret_io(proj_d, tables, norm_g, gmat, rows):
    cos, sin, decay, zeta, xi, gamma_c, bd = tables
    small = [decay, zeta, xi, gamma_c, bd, norm_g, gmat]
    return (_ret_steps, [proj_d] * 4 + [cos, sin] + small,
            [_col_spec(rows, 128, 0), _col_spec(rows, 128, 1), _col_spec(rows, GROUP_W, 1),
             _col_spec(rows, GROUP_W, 2),
             pl.BlockSpec((rows, 128), lambda b, t: (t, 0)), pl.BlockSpec((rows, 128), lambda b, t: (t, 0))]
            + [_whole_spec(a) for a in small],
            [pltpu.VMEM((2 * HEAD_W, GROUP_W), F32)])


def _mix_weights(w_in):
    w_in = _bf(w_in)
    a_w = w_in[..., 0:1024]
    b0 = 1024
    b_main = w_in[..., b0:b0 + 1024]
    b_a = jnp.repeat(w_in[..., b0 + 1024:b0 + 1028], HEAD_W, axis=-1)
    b_b = jnp.repeat(w_in[..., b0 + 1028:b0 + 1032], HEAD_W, axis=-1)
    c0 = b0 + 1032
    c_w = w_in[..., c0:c0 + 768]
    d0 = c0 + 768
    perm = _ret_perm()
    d_q = w_in[..., d0:d0 + 128][..., perm]
    d_k = w_in[..., d0 + 128:d0 + 256][..., perm]
    d_rest = w_in[..., d0 + 256:d0 + 768]
    b_w = jnp.concatenate([b_main, b_a, b_b], axis=-1)
    d_w = jnp.concatenate([d_q, d_k, d_rest], axis=-1)
    return a_w, b_w, c_w, d_w


def kernel(x, ffn1_norm, ffn1_w_in, ffn1_w_out, mix_norm, mix_w_in, mix_w_out, ffn2_norm, ffn2_w_in,
           ffn2_w_out, hgrn_lb_logits, hgrn_out_norm, gdn_conv_w, gdn_a_log, gdn_dt_bias, gdn_out_norm,
           sb_out_norm, ret_out_norm, final_norm):
    bsz, seq, d = x.shape
    depth = ffn1_norm.shape[0]
    n = bsz * seq
    assert ffn1_w_out.shape[1] % FFN_COLS == 0 and n % FFN_ROWS == 0 and n % PROJ_ROWS == 0
    assert seq % MIXER_ROWS == 0 and MIXER_ROWS % SB_BLOCK == 0 and MIXER_ROWS % RET_CHUNK == 0
    rows = MIXER_ROWS

    ctab = _cumsum_table()
    gmat = _head_block_matrix(GROUP_W, HEAD_W, GROUP_W, HEAD_W, BF16)
    bd = _head_block_matrix(GROUP_W, HEAD_W, GROUP_W, HEAD_W, F32)
    u2 = _sb_table()
    ret_tables = _retention_tables(seq, RET_CHUNK)
    row = lambda a: a.reshape(1, -1).astype(F32)
    rep = lambda a: jnp.repeat(a.astype(F32), HEAD_W).reshape(1, GROUP_W)

    w1_in, w1_out, w2_in, w2_out, w_mix_out = (_bf(w) for w in (ffn1_w_in, ffn1_w_out, ffn2_w_in, ffn2_w_out,
                                                                  mix_w_out))
    w_mix_in = _mix_weights(mix_w_in)

    x = x.reshape(n, d)
    for l in range(depth):
        x = _ffn(x, l, row(ffn1_norm[l]), w1_in, w1_out, tm=FFN_ROWS)
        pa, pb, pc, pd = _mix_proj(x, l, row(mix_norm[l]), w_mix_in, tm=PROJ_ROWS)
        shp = lambda a: a.reshape(bsz, seq, a.shape[-1])
        ya, yb, yc, yd = _run_mixers(
            [_hgrn_io(shp(pa), hgrn_lb_logits.astype(F32), row(hgrn_out_norm[l]), ctab, gmat, bd, l, rows),
             _gdn_io(shp(pb), gdn_conv_w[l].astype(F32), rep(gdn_a_log[l]), rep(gdn_dt_bias[l]),
                     row(gdn_out_norm[l]), ctab, gmat, bd, rows),
             _sb_io(shp(pc), row(sb_out_norm[l]), u2, gmat, rows),
             _ret_io(shp(pd), ret_tables, row(ret_out_norm[l]), gmat, rows)],
            bsz, seq, rows, "mixers")
        ys = [y.reshape(n, GROUP_W) for y in (ya, yb, yc, yd)]
        x = _ffn(x, l, row(ffn2_norm[l]), w2_in, w2_out, tm=FFN_ROWS, mix=(ys, w_mix_out),
                 final_gain=row(final_norm) if l == depth - 1 else None)
    return x.reshape(bsz, seq, d)
```

```python
import functools

import numpy as np
import jax
import jax.numpy as jnp
from jax import lax
from jax.experimental import pallas as pl
from jax.experimental.pallas import tpu as pltpu

F32 = jnp.float32
BF16 = jnp.bfloat16

EPS = 1e-6
CHUNK = 64
N_HEADS = 4
HEAD_W = 64
GROUP_W = N_HEADS * HEAD_W
D_DK = 32
ROPE_BASE = 10000.0
N_LEVELS = 6
VMEM_LIMIT = 56 * 1024 * 1024

FFN_ROWS = 1024
PROJ_ROWS = 512
MIXER_ROWS = 512
GDN_GROUP = 8
FFN_COLS = 256
RET_CHUNK = 128
SB_BLOCK = 256
SB_UNIT = 256
SB_DEAD = -151.0
LOG2_E = 1.4426950408889634


def _bf(x):
    return x.astype(BF16)


def _dot(a, b):
    return jnp.dot(a, b, preferred_element_type=F32)


def _dot_nt(a, b):
    return lax.dot_general(a, b, (((1,), (1,)), ((), ())), preferred_element_type=F32)


def _dot_tn(a, b):
    return lax.dot_general(a, b, (((0,), (0,)), ((), ())), preferred_element_type=F32)


def _split2(x):
    hi = _bf(x)
    lo = _bf(x - hi.astype(F32))
    return hi, lo


def _split3(x):
    h1 = _bf(x)
    r = x - h1.astype(F32)
    h2 = _bf(r)
    h3 = _bf(r - h2.astype(F32))
    return h1, h2, h3


def _select_rows(w3, x):
    return _dot(w3, jnp.concatenate(_split3(x), axis=0))


def _headsum(x, gmat):
    hi, lo = _split2(x)
    return _dot(hi, gmat) + _dot(lo, gmat)


def _head_rmsnorm(o, gmat, gain):
    ms = _headsum(o * o, gmat) * (1.0 / HEAD_W)
    return o * lax.rsqrt(ms + EPS) * gain


def _sigmoid(x):
    return 0.5 * jnp.tanh(0.5 * x) + 0.5


def _sigmoid_rel(x):
    return jnp.exp(jnp.minimum(x, 0.0) - jnp.log(1.0 + jnp.exp(-jnp.abs(x))))


def _silu(x):
    return x * _sigmoid(x)


def _softplus(x):
    return jnp.maximum(x, 0.0) + jnp.log(1.0 + jnp.exp(-jnp.abs(x)))


def _head_masks(width, lanes_per_head, dtype):
    lane = lax.broadcasted_iota(jnp.int32, (1, width), 1)
    return [((lane // lanes_per_head) == h).astype(dtype) for h in range(N_HEADS)]


def _stack_heads(x, masks):
    return jnp.concatenate([x * m for m in masks], axis=0)


def _level_masks(lane_stack=1):
    r = lax.broadcasted_iota(jnp.int32, (CHUNK, lane_stack * CHUNK), 0)
    c = lax.broadcasted_iota(jnp.int32, (CHUNK, lane_stack * CHUNK), 1) & (CHUNK - 1)
    pair = []
    for l in range(N_LEVELS):
        n = CHUNK >> l
        m = n // 2
        sh = N_LEVELS - l
        same = (r >> sh) == (c >> sh)
        pair.append((same & ((r & (n - 1)) >= m) & ((c & (n - 1)) < m)).astype(F32))
    eye = (r == c).astype(F32)
    return pair, eye


def _decay_factors(cs):
    return [(jnp.exp(cs[(2 + l) * CHUNK:(3 + l) * CHUNK]),
             jnp.exp(cs[(2 + N_LEVELS + l) * CHUNK:(3 + N_LEVELS + l) * CHUNK])) for l in range(N_LEVELS)]


def _cumsum_table():
    t = np.arange(CHUNK)
    tri = (t[None, :] <= t[:, None]).astype(np.float32)
    up = (t[None, :] > t[:, None]).astype(np.float32)
    q_blocks, k_blocks = [], []
    for l in range(N_LEVELS):
        n = CHUNK >> l
        m = n // 2
        anchor = (t // n) * n + m - 1
        upper = ((t % n) >= m)[:, None]
        q_blocks.append(np.where(upper, tri - tri[anchor], 0.0))
        k_blocks.append(np.where(~upper, tri[anchor] - tri, 0.0))
    tab = np.concatenate([tri, up] + q_blocks + k_blocks, axis=0)
    assert set(np.unique(tab)) <= {0.0, 1.0}
    return jnp.asarray(np.concatenate([tab, tab, tab], axis=1), dtype=BF16)


def _head_block_matrix(rows, rows_per_head, cols, cols_per_head, dtype):
    r = np.arange(rows)[:, None] // rows_per_head
    c = np.arange(cols)[None, :] // cols_per_head
    return jnp.asarray((r == c).astype(np.float32), dtype=dtype)


def _ret_lane_head(p):
    return (p % 64) // 16


def _retention_tables(seq, chunk):
    p = np.arange(128)
    half = D_DK // 2
    inv_freq = ROPE_BASE ** (-(p % 16).astype(np.float64) / half)
    ang = np.arange(seq, dtype=np.float64)[:, None] * inv_freq[None, :]
    cos = np.cos(ang)
    sin = np.sin(ang) * np.where(p < 64, -1.0, 1.0)[None, :]
    log_gamma = np.log(1.0 - 2.0 ** (-5.0 - np.arange(N_HEADS, dtype=np.float64)))
    c = np.arange(chunk, dtype=np.float64)
    rel = c[:, None] - c[None, :]
    decay = np.where(rel[None] >= 0, np.exp(rel[None] * log_gamma[:, None, None]), 0.0)
    decay = decay.transpose(1, 0, 2).reshape(chunk, N_HEADS * chunk)
    lane_h = _ret_lane_head(p)
    zeta = np.exp((chunk - 1 - c)[:, None] * log_gamma[lane_h][None, :])
    xi = np.exp((c + 1.0)[:, None] * log_gamma[lane_h][None, :])
    gamma_c = np.exp(chunk * log_gamma)[np.arange(GROUP_W) // HEAD_W][None, :]
    bd = (lane_h[:, None] == (np.arange(GROUP_W) // HEAD_W)[None, :]).astype(np.float32)
    f = lambda a: jnp.asarray(a, dtype=F32)
    return f(cos), f(sin), f(decay), f(zeta), f(xi), f(gamma_c), f(bd)


def _ret_perm():
    p = np.arange(128)
    h = _ret_lane_head(p)
    return h * D_DK + (p % 16) + np.where(p >= 64, 16, 0)


def _rms_rows(x, gain):
    return x * lax.rsqrt(jnp.mean(x * x, axis=-1, keepdims=True) + EPS) * gain


def _ffn_body(has_mix, has_final, *refs):
    refs = list(refs)
    x_ref = refs.pop(0)
    y_refs = [refs.pop(0) for _ in range(4)] if has_mix else []
    wmix_ref = refs.pop(0) if has_mix else None
    g_ref, win_ref, wout_ref = refs.pop(0), refs.pop(0), refs.pop(0)
    gf_ref = refs.pop(0) if has_final else None
    o_ref = refs.pop(0)

    x = x_ref[...]
    for m, y_ref in enumerate(y_refs):
        x = x + _dot(y_ref[...], wmix_ref[m * GROUP_W:(m + 1) * GROUP_W, :])
    h = _bf(_rms_rows(x, g_ref[...]))

    d_ff = wout_ref.shape[0]
    steps = d_ff // FFN_COLS
    acc = None
    act = None
    for c in range(steps + 1):
        if c < steps:
            gate = _dot(h, win_ref[:, c * FFN_COLS:(c + 1) * FFN_COLS])
            up = _dot(h, win_ref[:, d_ff + c * FFN_COLS:d_ff + (c + 1) * FFN_COLS])
        if act is not None:
            down = _dot(act, wout_ref[(c - 1) * FFN_COLS:c * FFN_COLS, :])
            acc = down if acc is None else acc + down
        if c < steps:
            act = _bf(_silu(gate) * up)
    y = x + 0.5 * acc
    if has_final:
        y = _rms_rows(y, gf_ref[...])
    o_ref[...] = y


def _layer_spec(a, layer):
    return pl.BlockSpec((None,) + a.shape[1:], lambda i: (layer, 0, 0), pipeline_mode=pl.Buffered(1))


def _ffn(x, layer, gain, w_in, w_out, *, tm, mix=None, final_gain=None):
    n, d = x.shape
    rows = lambda w: pl.BlockSpec((tm, w), lambda i: (i, 0))
    whole = lambda a: _layer_spec(a, layer) if a.ndim == 3 else pl.BlockSpec(a.shape, lambda i: (0, 0))
    args, specs = [x], [rows(d)]
    if mix is not None:
        ys, w_mix = mix
        args += list(ys) + [w_mix]
        specs += [rows(GROUP_W) for _ in ys] + [whole(w_mix)]
    args += [gain, w_in, w_out]
    specs += [whole(gain), whole(w_in), whole(w_out)]
    if final_gain is not None:
        args.append(final_gain)
        specs.append(whole(final_gain))
    return pl.pallas_call(
        functools.partial(_ffn_body, mix is not None, final_gain is not None),
        grid=(n // tm,),
        in_specs=specs,
        out_specs=rows(d),
        out_shape=jax.ShapeDtypeStruct((n, d), F32),
        compiler_params=pltpu.CompilerParams(
            dimension_semantics=("parallel",), vmem_limit_bytes=VMEM_LIMIT),
        name="ffn",
    )(*args)


def _proj_body(x_ref, g_ref, wa_ref, wb_ref, wc_ref, wd_ref, oa_ref, ob_ref, oc_ref, od_ref):
    h = _bf(_rms_rows(x_ref[...], g_ref[...]))
    oa_ref[...] = _dot(h, wa_ref[...])
    ob_ref[...] = _dot(h, wb_ref[...])
    oc_ref[...] = _bf(_dot(h, wc_ref[...]))
    od_ref[...] = _dot(h, wd_ref[...])


def _mix_proj(x, layer, gain, ws, *, tm):
    n, d = x.shape
    dts = (F32, F32, BF16, F32)
    return pl.pallas_call(
        _proj_body,
        grid=(n // tm,),
        in_specs=[pl.BlockSpec((tm, d), lambda i: (i, 0)), pl.BlockSpec((1, d), lambda i: (0, 0))]
        + [_layer_spec(w, layer) for w in ws],
        out_specs=[pl.BlockSpec((tm, w.shape[2]), lambda i: (i, 0)) for w in ws],
        out_shape=[jax.ShapeDtypeStruct((n, w.shape[2]), dt) for w, dt in zip(ws, dts)],
        compiler_params=pltpu.CompilerParams(
            dimension_semantics=("parallel",), vmem_limit_bytes=VMEM_LIMIT),
        name="mix_proj",
    )(x, gain, *ws)


def _hgrn_steps(layer, q_ref, f_ref, i_ref, g_ref, lbl_ref, ng_ref, ctab_ref, gmat_ref, bd_ref,
                o_ref, st_scr):
    @pl.when(pl.program_id(1) == 0)
    def _():
        st_scr[...] = jnp.zeros_like(st_scr)

    logits = lbl_ref[...]
    e = jnp.exp(logits - jnp.max(logits, axis=0, keepdims=True))
    p = e / jnp.sum(e, axis=0, keepdims=True)
    lb = jnp.zeros_like(p[0:1])
    for r in range(1, layer + 1):
        lb = lb + p[r:r + 1]

    pair, eye = _level_masks(lane_stack=N_HEADS)
    hm = _head_masks(GROUP_W, HEAD_W, BF16)
    ctab = ctab_ref[...]
    gmat = gmat_ref[...]
    bd = bd_ref[...]
    ng = ng_ref[...]

    chunks = range(q_ref.shape[1] // CHUNK)
    rows_of = lambda c: pl.ds(c * CHUNK, CHUNK)
    qs, kk, cs = [], [], []
    for c in chunks:
        f = lb + (1.0 - lb) * _sigmoid_rel(f_ref[0, rows_of(c), :])
        qs.append(_silu(q_ref[0, rows_of(c), :]))
        kk.append(1.0 - f)
        cs.append(_select_rows(ctab, jnp.log(f)))
        yield 250

    scores = []
    for c in chunks:
        sc = eye * _dot_nt(_bf(qs[c]), _stack_heads(_bf(kk[c]), hm))
        for l, (fq, fk) in enumerate(_decay_factors(cs[c])):
            sc = sc + pair[l] * _dot_nt(_bf(qs[c] * fq), _stack_heads(_bf(kk[c] * fk), hm))
        scores.append(_bf(sc))
        yield 450

    qdec, b_last, o_intra, st_add = [], [], [], []
    for c in chunks:
        b = cs[c][0:CHUNK]
        rem = cs[c][CHUNK:2 * CHUNK]
        v = _bf(_silu(i_ref[0, rows_of(c), :]))
        o_intra.append(_dot(scores[c], _stack_heads(v, hm)))
        qdec.append(_bf(qs[c] * jnp.exp(b)))
        st_add.append(_dot_tn(v, _bf(kk[c] * jnp.exp(rem))) * bd)
        b_last.append(jnp.exp(b[CHUNK - 1:CHUNK]))
        yield 200

    st = st_scr[...]
    outs = []
    for c in chunks:
        outs.append(o_intra[c] + _dot_nt(qdec[c], _bf(st)))
        st = st * b_last[c] + st_add[c]
        yield 100
    st_scr[...] = st
    o_ref[0] = _bf(_head_rmsnorm(jnp.concatenate(outs, axis=0), gmat, ng) * _sigmoid(g_ref[0]))
    yield 150


def _weave(gens):
    spent = [0.0] * len(gens)
    live = list(range(len(gens)))
    while live:
        i = min(live, key=lambda j: spent[j])
        try:
            spent[i] += next(gens[i])
        except StopIteration:
            live.remove(i)


def _whole_spec(a):
    return pl.BlockSpec(a.shape, lambda b, t: (0,) * a.ndim)


def _col_spec(rows, width, c):
    return pl.BlockSpec((1, rows, width), lambda b, t: (b, t, c))


def _run_mixers(mixers, bsz, seq, rows, name):
    n_in = [len(m[1]) for m in mixers]
    n_scr = [len(m[3]) for m in mixers]

    def body(*refs):
        ins, outs, scr = refs[:sum(n_in)], refs[sum(n_in):sum(n_in) + len(mixers)], refs[sum(n_in) + len(mixers):]
        gens = []
        for i, m in enumerate(mixers):
            a, s = sum(n_in[:i]), sum(n_scr[:i])
            gens.append(m[0](*ins[a:a + n_in[i]], outs[i], *scr[s:s + n_scr[i]]))
        _weave(gens)

    out = pl.pallas_call(
        body,
        grid=(bsz, seq // rows),
        in_specs=[s for m in mixers for s in m[2]],
        out_specs=[pl.BlockSpec((1, rows, GROUP_W), lambda b, t: (b, t, 0)) for _ in mixers],
        out_shape=[jax.ShapeDtypeStruct((bsz, seq, GROUP_W), BF16) for _ in mixers],
        scratch_shapes=[s for m in mixers for s in m[3]],
        compiler_params=pltpu.CompilerParams(
            dimension_semantics=("parallel", "arbitrary"), vmem_limit_bytes=VMEM_LIMIT),
        name=name,
    )(*[a for m in mixers for a in m[1]])
    return out


def _hgrn_io(proj_a, lb_logits, norm_g, ctab, gmat, bd, layer, rows):
    small = [lb_logits, norm_g, ctab, gmat, bd]
    return (functools.partial(_hgrn_steps, layer), [proj_a] * 4 + small,
            [_col_spec(rows, GROUP_W, c) for c in range(4)] + [_whole_spec(a) for a in small],
            [pltpu.VMEM((GROUP_W, GROUP_W), F32)])


def _gdn_steps(q_ref, k_ref, v_ref, g_ref, a_ref, bl_ref, cw_ref, alog_ref, dtb_ref, ng_ref,
               ctab_ref, gmat_ref, bd_ref, o_ref, s_scr, tail_scr, ext_scr):
    @pl.when(pl.program_id(1) == 0)
    def _():
        s_scr[...] = jnp.zeros_like(s_scr)
        tail_scr[...] = jnp.zeros_like(tail_scr)

    pair, eye = _level_masks()
    hm = _head_masks(GROUP_W, HEAD_W, BF16)
    hmf = _head_masks(GROUP_W, HEAD_W, F32)
    ctab = ctab_ref[...]
    gmat = gmat_ref[...]
    bd = bd_ref[...]
    ng = ng_ref[...]
    neg_a = -jnp.exp(alog_ref[...])
    dtb = dtb_ref[...]
    raw = (q_ref, k_ref, v_ref)
    rows = q_ref.shape[1]

    for j in range(3):
        ext_scr[0:8, j * GROUP_W:(j + 1) * GROUP_W] = tail_scr[:, j * GROUP_W:(j + 1) * GROUP_W]
        ext_scr[8:8 + CHUNK, j * GROUP_W:(j + 1) * GROUP_W] = raw[j][0, 0:CHUNK, :]
        tail_scr[:, j * GROUP_W:(j + 1) * GROUP_W] = raw[j][0, rows - 8:rows, :]

    def conv(c, j):
        w = cw_ref[:, j * GROUP_W:(j + 1) * GROUP_W]
        acc = None
        for d in range(4):
            if c == 0:
                x = ext_scr[pl.ds(8 - d, CHUNK), j * GROUP_W:(j + 1) * GROUP_W]
            else:
                x = raw[j][0, pl.ds(c * CHUNK - d, CHUNK), :]
            term = x * w[3 - d:4 - d, :]
            acc = term if acc is None else acc + term
        return _silu(acc)

    nch = rows // CHUNK
    rr = lax.broadcasted_iota(jnp.int32, (N_HEADS * CHUNK, CHUNK), 0) & (CHUNK - 1)
    cc = lax.broadcasted_iota(jnp.int32, (N_HEADS * CHUNK, CHUNK), 1)
    tril = rr >= cc
    tril_f = tril.astype(F32)
    stril_f = (rr > cc).astype(F32)
    sub = lax.broadcasted_iota(jnp.int32, (1, GROUP_W), 1) & (HEAD_W - 1)
    lane_is = [(sub == i).astype(BF16) for i in range(6)]
    qc, kc, vc, qn, kn, cs, diff, prod, kb, beta = ({} for _ in range(10))
    qg, kdec, vb, kbg, g_last, attn, m_mat, t_inv = ({} for _ in range(8))
    o_lhs, o_add, s_lhs, s_add = {}, {}, {}, {}

    def phase1(grp):
        for c in grp:
            qc[c] = conv(c, 0)
            kc[c] = conv(c, 1)
            vc[c] = conv(c, 2)
            qn[c] = _headsum(qc[c] * qc[c], gmat)
            kn[c] = _headsum(kc[c] * kc[c], gmat)
            log_alpha = neg_a * _softplus(a_ref[0, pl.ds(c * CHUNK, CHUNK), :] + dtb)
            cs[c] = _select_rows(ctab[0:2 * CHUNK], log_alpha)
            yield
        for c in grp:
            qc[c] = qc[c] * lax.rsqrt(qn[c] + EPS) * (HEAD_W ** -0.5)
            kc[c] = kc[c] * lax.rsqrt(kn[c] + EPS)
            beta[c] = _sigmoid(bl_ref[0, pl.ds(c * CHUNK, CHUNK), :])
            kb[c] = kc[c] * beta[c]
            g1, g2, g3 = _split3(cs[c][0:CHUNK])
            lhs = g1 * lane_is[0] + g2 * lane_is[1] + g3 * lane_is[2] + (lane_is[3] + lane_is[4] + lane_is[5])
            rhs = (lane_is[0] + lane_is[1] + lane_is[2]) - g1 * lane_is[3] - g2 * lane_is[4] - g3 * lane_is[5]
            diff[c] = _dot_nt(_stack_heads(lhs, hm), rhs)
            prod[c] = _dot_nt(jnp.concatenate([_stack_heads(_bf(qc[c]), hm), _stack_heads(_bf(kb[c]), hm)],
                                              axis=0), _bf(kc[c]))
            yield
        for c in grp:
            gc = cs[c][0:CHUNK]
            rem = cs[c][CHUNK:2 * CHUNK]
            dec = jnp.exp(jnp.where(tril, diff[c], 0.0))
            attn_st = _bf(tril_f * dec * prod[c][0:N_HEADS * CHUNK])
            m_st = stril_f * dec * prod[c][N_HEADS * CHUNK:2 * N_HEADS * CHUNK]
            for h in range(N_HEADS):
                attn[c, h] = attn_st[h * CHUNK:(h + 1) * CHUNK]
                m_mat[c, h] = m_st[h * CHUNK:(h + 1) * CHUNK]
            qg[c] = qc[c] * jnp.exp(gc)
            kdec[c] = _bf(kc[c] * jnp.exp(rem))
            vb[c] = vc[c] * beta[c]
            kbg[c] = kb[c] * jnp.exp(gc)
            g_last[c] = jnp.exp(gc[CHUNK - 1:CHUNK])
            yield

    def phase2(grp):
        pairs = [(c, h) for c in grp for h in range(N_HEADS)]
        for p in pairs:
            t_inv[p] = eye - pair[N_LEVELS - 1] * m_mat[p]
        for l in range(N_LEVELS - 2, -1, -1):
            t_b = {p: _bf(t_inv[p]) for p in pairs}
            tl = {p: _dot(t_b[p], _bf(pair[l] * m_mat[p])) for p in pairs}
            yield
            for p in pairs:
                t_inv[p] = t_inv[p] - _dot(_bf(tl[p]), t_b[p])
            yield

    def phase3(grp):
        uw = {}
        for c in grp:
            rhs = jnp.concatenate([vb[c], kbg[c]], axis=1)
            acc = None
            for h in range(N_HEADS):
                rhs_h = _bf(rhs * jnp.concatenate([hmf[h], hmf[h]], axis=1))
                term = _dot(_bf(t_inv[c, h]), rhs_h)
                acc = term if acc is None else acc + term
            uw[c] = _bf(acc)
            yield
        for c in grp:
            u_b = uw[c][:, 0:GROUP_W]
            w_b = uw[c][:, GROUP_W:2 * GROUP_W]
            aw = None
            for h in range(N_HEADS):
                term = _dot(attn[c, h], jnp.concatenate([u_b * hm[h], w_b * hm[h]], axis=1))
                aw = term if aw is None else aw + term
            o_add[c] = aw[:, 0:GROUP_W]
            o_lhs[c] = _bf(qg[c] - aw[:, GROUP_W:2 * GROUP_W])
            ks = _dot_tn(kdec[c], uw[c])
            s_add[c] = ks[:, 0:GROUP_W] * bd
            s_lhs[c] = _bf(-ks[:, GROUP_W:2 * GROUP_W] * bd)
            yield

    state = [s_scr[...]]
    outs = []

    def phase4(grp):
        for c in grp:
            s_b = _bf(state[0])
            outs.append(o_add[c] + _dot(o_lhs[c], s_b))
            state[0] = state[0] * g_last[c] + _dot(s_lhs[c], s_b) + s_add[c]
            yield

    def chain(*gens):
        for gen in gens:
            yield from gen

    def run(primary, secondary=iter(()), per_step=1):
        for _ in primary:
            for _ in range(per_step):
                next(secondary, None)
            yield 400
        for _ in secondary:
            yield 200

    groups = [range(g, min(g + GDN_GROUP, nch)) for g in range(0, nch, GDN_GROUP)]
    yield from run(phase1(groups[0]))
    for gi, grp in enumerate(groups):
        fill = []
        if gi > 0:
            fill += [phase3(groups[gi - 1]), phase4(groups[gi - 1])]
        if gi + 1 < len(groups):
            fill.append(phase1(groups[gi + 1]))
        yield from run(phase2(grp), chain(*fill), per_step=2)
    yield from run(chain(phase3(groups[-1]), phase4(groups[-1])))
    s_scr[...] = state[0]
    o_ref[0] = _bf(_head_rmsnorm(jnp.concatenate(outs, axis=0), gmat, ng) * _silu(g_ref[0]))
    yield 200


def _gdn_io(proj_b, conv_w, a_log_rep, dt_bias_rep, norm_g, ctab, gmat, bd, rows):
    small = [conv_w, a_log_rep, dt_bias_rep, norm_g, ctab, gmat, bd]
    return (_gdn_steps, [proj_b] * 6 + small,
            [_col_spec(rows, GROUP_W, c) for c in range(6)] + [_whole_spec(a) for a in small],
            [pltpu.VMEM((GROUP_W, GROUP_W), F32),
             pltpu.VMEM((8, 3 * GROUP_W), F32),
             pltpu.VMEM((8 + CHUNK, 3 * GROUP_W), F32)])


def _sb_steps(q_ref, k_ref, v_ref, ng_ref, later_ref, gmat_ref, o_ref):
    blk, unit = SB_BLOCK, SB_UNIT
    assert blk == unit
    n_blk = q_ref.shape[1] // blk
    hm = _head_masks(GROUP_W, HEAD_W, BF16)
    later = later_ref[...]
    t_loc = lax.broadcasted_iota(jnp.int32, (N_HEADS * blk // 2, unit), 0) & (blk - 1)
    s_loc = lax.broadcasted_iota(jnp.int32, (N_HEADS * blk // 2, unit), 1)
    past0 = t_loc > s_loc

    def scores(qs, u, past):
        kblk = k_ref[0, pl.ds(u * unit, unit), :]
        args, totals = [], []
        for q_half in qs:
            z2 = _dot_nt(q_half, kblk) * LOG2_E
            log_sig = jnp.minimum(z2, 0.0) - jnp.log2(1.0 + jnp.exp2(-jnp.abs(z2)))
            log_stay = log_sig - z2
            if past is not None:
                log_stay = jnp.where(past, log_stay, 0.0)
            args.append(log_sig + _dot(_bf(log_stay), later))
            totals.append(jnp.sum(log_stay, axis=-1, keepdims=True))
        return args, totals

    def weights(args, carries, past):
        halves = []
        for arg, carry in zip(args, carries):
            wts = jnp.exp2(arg + carry)
            if past is not None:
                wts = jnp.where(past, wts, 0.0)
            halves.append(_bf(wts))
        per_half = N_HEADS // 2
        return jnp.concatenate([halves[h // per_half][(h % per_half) * blk:(h % per_half + 1) * blk]
                                for h in range(N_HEADS)], axis=1)

    def values(u, scale=None):
        vblk = v_ref[0, pl.ds(u * unit, unit), :]
        if scale is not None:
            vblk = vblk * scale
        return jnp.concatenate([vblk * hm[h] for h in range(N_HEADS)], axis=0)

    started = []
    for j in range(n_blk):
        u0 = pl.program_id(1) * n_blk + j
        q = q_ref[0, j * blk:(j + 1) * blk, :] * (HEAD_W ** -0.5)
        qs = [jnp.concatenate([q * hm[h] for h in (2 * f, 2 * f + 1)], axis=0) for f in range(2)]
        has_prev = (u0 > 0).astype(F32)
        arg0, total0 = scores(qs, u0, past0)
        yield 900
        arg1, total1 = scores(qs, jnp.maximum(u0 - 1, 0), None)
        yield 900
        w0 = weights(arg0, [0.0, 0.0], past0)
        w1 = weights(arg1, total0, None)
        yield 500
        v1 = values(jnp.maximum(u0 - 1, 0), jnp.broadcast_to(has_prev, (1, GROUP_W)).astype(BF16))
        acc = _dot(jnp.concatenate([w0, w1], axis=1), jnp.concatenate([values(u0), v1], axis=0))
        started.append((qs, u0, tuple(a + b * has_prev for a, b in zip(total0, total1)), acc))
        yield 500

    yield float("inf")

    for j, (qs, u0, carry, acc) in enumerate(started):
        def live(state):
            return jnp.logical_and(state[0] >= 0, state[1])

        def step(state, qs=qs):
            u, _, carry, acc = state
            arg, total = scores(qs, u, None)
            acc = acc + _dot(weights(arg, carry, None), values(u))
            carry = tuple(a + b for a, b in zip(carry, total))
            return u - 1, top(carry) > SB_DEAD, carry, acc

        top = lambda cs: jnp.maximum(jnp.max(cs[0]), jnp.max(cs[1]))
        _, _, carry, acc = lax.while_loop(live, step, (u0 - 2, top(carry) > SB_DEAD, carry, acc))
        o_ref[0, j * blk:(j + 1) * blk, :] = _bf(_head_rmsnorm(acc, gmat_ref[...], ng_ref[...]))


def _sb_io(proj_c, norm_g, later, gmat, rows):
    seq = proj_c.shape[1]
    small = [norm_g, later, gmat]
    return (_sb_steps, [proj_c] * 3 + small,
            [_col_spec(rows, GROUP_W, 0),
             pl.BlockSpec((1, seq, GROUP_W), lambda b, t: (b, 0, 1)),
             pl.BlockSpec((1, seq, GROUP_W), lambda b, t: (b, 0, 2))] + [_whole_spec(a) for a in small],
            [])


def _sb_table():
    j = np.arange(SB_UNIT)[:, None]
    s = np.arange(SB_UNIT)[None, :]
    return jnp.asarray((j > s).astype(np.float32), dtype=BF16)


def _ret_steps(q_ref, k_ref, v_ref, g_ref, cos_ref, sin_ref, decay_ref, zeta_ref, xi_ref, gc_ref,
               bd_ref, ng_ref, gmat_ref, o_ref, s_scr):
    @pl.when(pl.program_id(1) == 0)
    def _():
        s_scr[...] = jnp.zeros_like(s_scr)

    chunk = decay_ref.shape[0]
    decay = decay_ref[...]
    lane = lax.broadcasted_iota(jnp.int32, (1, 2 * HEAD_W), 1)
    hq = [(((lane % 64) // 16) == h).astype(BF16) for h in range(N_HEADS)]
    hv = _head_masks(GROUP_W, HEAD_W, BF16)
    gmat = gmat_ref[...]
    ng = ng_ref[...]
    bd = bd_ref[...]
    zeta = zeta_ref[...]
    xi = xi_ref[...]
    gamma_c = gc_ref[...]

    nch = q_ref.shape[1] // chunk
    qx, o_intra, s_add = [], [], []
    for c in range(nch):
        sl = pl.ds(c * chunk, chunk)
        cos = cos_ref[sl, :]
        sin = sin_ref[sl, :]
        q = q_ref[0, sl, :]
        k = k_ref[0, sl, :]
        qr = q * cos + pltpu.roll(q, 64, axis=1) * sin
        kr = (k * cos + pltpu.roll(k, 64, axis=1) * sin) * (D_DK ** -0.5)
        v = _bf(v_ref[0, sl, :])
        sc = _bf(_dot_nt(_bf(qr), _stack_heads(_bf(kr), hq)) * decay)
        o_intra.append(_dot(sc, _stack_heads(v, hv)))
        qx.append(_bf(qr * xi))
        s_add.append(_dot_tn(_bf(kr * zeta), v) * bd)
        yield 350

    s = s_scr[...]
    outs = []
    for c in range(nch):
        outs.append(o_intra[c] + _dot(qx[c], _bf(s)))
        s = s * gamma_c + s_add[c]
        yield 80
    s_scr[...] = s
    o_ref[0] = _bf(_head_rmsnorm(jnp.concatenate(outs, axis=0), gmat, ng) * _silu(g_ref[0]))
    yield 150


def _ret_io(proj_d, tables, norm_g, gmat, rows):
    cos, sin, decay, zeta, xi, gamma_c, bd = tables
    small = [decay, zeta, xi, gamma_c, bd, norm_g, gmat]
    return (_ret_steps, [proj_d] * 4 + [cos, sin] + small,
            [_col_spec(rows, 128, 0), _col_spec(rows, 128, 1), _col_spec(rows, GROUP_W, 1),
             _col_spec(rows, GROUP_W, 2),
             pl.BlockSpec((rows, 128), lambda b, t: (t, 0)), pl.BlockSpec((rows, 128), lambda b, t: (t, 0))]
            + [_whole_spec(a) for a in small],
            [pltpu.VMEM((2 * HEAD_W, GROUP_W), F32)])


def _mix_weights(w_in):
    w_in = _bf(w_in)
    a_w = w_in[..., 0:1024]
    b0 = 1024
    b_main = w_in[..., b0:b0 + 1024]
    b_a = jnp.repeat(w_in[..., b0 + 1024:b0 + 1028], HEAD_W, axis=-1)
    b_b = jnp.repeat(w_in[..., b0 + 1028:b0 + 1032], HEAD_W, axis=-1)
    c0 = b0 + 1032
    c_w = w_in[..., c0:c0 + 768]
    d0 = c0 + 768
    perm = _ret_perm()
    d_q = w_in[..., d0:d0 + 128][..., perm]
    d_k = w_in[..., d0 + 128:d0 + 256][..., perm]
    d_rest = w_in[..., d0 + 256:d0 + 768]
    b_w = jnp.concatenate([b_main, b_a, b_b], axis=-1)
    d_w = jnp.concatenate([d_q, d_k, d_rest], axis=-1)
    return a_w, b_w, c_w, d_w


def kernel(x, ffn1_norm, ffn1_w_in, ffn1_w_out, mix_norm, mix_w_in, mix_w_out, ffn2_norm, ffn2_w_in,
           ffn2_w_out, hgrn_lb_logits, hgrn_out_norm, gdn_conv_w, gdn_a_log, gdn_dt_bias, gdn_out_norm,
           sb_out_norm, ret_out_norm, final_norm):
    bsz, seq, d = x.shape
    depth = ffn1_norm.shape[0]
    n = bsz * seq
    assert ffn1_w_out.shape[1] % FFN_COLS == 0 and n % FFN_ROWS == 0 and n % PROJ_ROWS == 0
    assert seq % MIXER_ROWS == 0 and MIXER_ROWS % SB_BLOCK == 0 and MIXER_ROWS % RET_CHUNK == 0
    rows = MIXER_ROWS

    ctab = _cumsum_table()
    gmat = _head_block_matrix(GROUP_W, HEAD_W, GROUP_W, HEAD_W, BF16)
    bd = _head_block_matrix(GROUP_W, HEAD_W, GROUP_W, HEAD_W, F32)
    u2 = _sb_table()
    ret_tables = _retention_tables(seq, RET_CHUNK)
    row = lambda a: a.reshape(1, -1).astype(F32)
    rep = lambda a: jnp.repeat(a.astype(F32), HEAD_W).reshape(1, GROUP_W)

    w1_in, w1_out, w2_in, w2_out, w_mix_out = (_bf(w) for w in (ffn1_w_in, ffn1_w_out, ffn2_w_in, ffn2_w_out,
                                                                  mix_w_out))
    w_mix_in = _mix_weights(mix_w_in)

    x = x.reshape(n, d)
    for l in range(depth):
        x = _ffn(x, l, row(ffn1_norm[l]), w1_in, w1_out, tm=FFN_ROWS)
        pa, pb, pc, pd = _mix_proj(x, l, row(mix_norm[l]), w_mix_in, tm=PROJ_ROWS)
        shp = lambda a: a.reshape(bsz, seq, a.shape[-1])
        ya, yb, yc, yd = _run_mixers(
            [_hgrn_io(shp(pa), hgrn_lb_logits.astype(F32), row(hgrn_out_norm[l]), ctab, gmat, bd, l, rows),
             _gdn_io(shp(pb), gdn_conv_w[l].astype(F32), rep(gdn_a_log[l]), rep(gdn_dt_bias[l]),
                     row(gdn_out_norm[l]), ctab, gmat, bd, rows),
             _sb_io(shp(pc), row(sb_out_norm[l]), u2, gmat, rows),
             _ret_io(shp(pd), ret_tables, row(ret_out_norm[l]), gmat, rows)],
            bsz, seq, rows, "mixers")
        ys = [y.reshape(n, GROUP_W) for y in (ya, yb, yc, yd)]
        x = _ffn(x, l, row(ffn2_norm[l]), w2_in, w2_out, tm=FFN_ROWS, mix=(ys, w_mix_out),
                 final_gain=row(final_norm) if l == depth - 1 else None)
    return x.reshape(bsz, seq, d)
```
